```python
import jax, jax.numpy as jnp
from jax import lax
import numpy as np

D_MODEL = 1024
BATCH = 2
SEQ = 16384
DEPTH = 4
DEC_BATCH = 16
DEC_SEQ = 16
PAST_LEN = 1024

CHUNK = 64
N_MIXERS = 2
N_GLA = (DEPTH + 1) // 2
N_CONV = DEPTH // 2
N_HEADS = 4
DK_TOT = D_MODEL // 2
DV_TOT = 2 * D_MODEL
HEAD_K = DK_TOT // N_HEADS
HEAD_V = DV_TOT // N_HEADS
GATE_RANK = 16
GATE_NORM = 16.0
D_INNER = 2 * D_MODEL
CONV_W = 3
EPS = 1e-6

kernel_name = "hybrid_gla_shortconv_stream_step"


def rms_norm(x, g):
    xf = x.astype(jnp.float32)
    y = xf * lax.rsqrt(jnp.mean(xf * xf, axis=-1, keepdims=True) + EPS)
    return (y * g.astype(jnp.float32)).astype(x.dtype)


def gla_recurrence(q, k, v, log_a, s0):
    b_sz, t_len = q.shape[0], q.shape[1]
    n_blk = -(-t_len // CHUNK)
    pad = n_blk * CHUNK - t_len

    def prep(a):
        a = jnp.pad(a.astype(jnp.float32), ((0, 0), (0, pad), (0, 0), (0, 0)))
        return a.reshape(b_sz, n_blk, CHUNK, N_HEADS, -1).transpose(1, 0, 3, 2, 4)

    qs, ks, vs, gs = prep(q), prep(k), prep(v), prep(log_a)
    causal = jnp.tril(jnp.ones((CHUNK, CHUNK), dtype=bool))[:, :, None]

    def step(s, xs):
        qc, kc, vc, gc = xs
        bcum = jnp.cumsum(gc, axis=2)
        o_inter = jnp.einsum('bhtk,bhkv->bhtv', qc * jnp.exp(bcum), s)
        diff = bcum[:, :, :, None, :] - bcum[:, :, None, :, :]
        decay = jnp.exp(jnp.where(causal, diff, -jnp.inf))
        att = jnp.einsum('bhtk,bhsk,bhtsk->bhts', qc, kc, decay)
        o = o_inter + jnp.einsum('bhts,bhsv->bhtv', att, vc)
        b_last = bcum[:, :, -1:, :]
        s_new = jnp.exp(b_last[:, :, 0, :])[..., None] * s + jnp.einsum(
            'bhsk,bhsv->bhkv', kc * jnp.exp(b_last - bcum), vc)
        return s_new, o

    s_fin, o = lax.scan(step, s0.astype(jnp.float32), (qs, ks, vs, gs))
    o = o.transpose(1, 0, 3, 2, 4).reshape(b_sz, n_blk * CHUNK, N_HEADS, HEAD_V)[:, :t_len]
    return o, s_fin


def gla_mixer(h, s0, w_in, w_f1, w_f2, b_f, g_head, w_out):
    b_sz, t_len, _ = h.shape
    proj = h @ w_in
    q, k, v, r = jnp.split(proj, [DK_TOT, 2 * DK_TOT, 2 * DK_TOT + DV_TOT], axis=-1)
    log_a = jax.nn.log_sigmoid(((h @ w_f1) @ w_f2 + b_f).astype(jnp.float32)) / GATE_NORM
    q = q.reshape(b_sz, t_len, N_HEADS, HEAD_K) * (HEAD_K ** -0.5)
    k = k.reshape(b_sz, t_len, N_HEADS, HEAD_K)
    v = v.reshape(b_sz, t_len, N_HEADS, HEAD_V)
    log_a = log_a.reshape(b_sz, t_len, N_HEADS, HEAD_K)
    o, s_fin = gla_recurrence(q, k, v, log_a, s0)
    o = rms_norm(o, g_head.reshape(N_HEADS, HEAD_V))
    o = o.reshape(b_sz, t_len, DV_TOT).astype(h.dtype) * jax.nn.silu(r)
    return o @ w_out, s_fin.astype(s0.dtype)


def conv_mixer(h, hist, w_in, conv_w, w_out):
    t_len = h.shape[1]
    bg, cg, u, z = jnp.split(h @ w_in, 4, axis=-1)
    xc = cg * u
    xpad = jnp.concatenate([hist.astype(xc.dtype), xc], axis=1)
    conv = (xpad[:, 0:t_len] * conv_w[0] + xpad[:, 1:t_len + 1] * conv_w[1]
            + xpad[:, 2:t_len + 2] * conv_w[2])
    y = bg * conv * jax.nn.silu(z)
    return y @ w_out, xpad[:, -(CONV_W - 1):]


def setup_inputs(seed: int = 0) -> dict:
    key = jax.random.key(seed)
    ks = jax.random.split(key, 16)
    f32 = jnp.float32
    nrm = lambda k, shape, s: jax.random.normal(k, shape, f32) * s
    return {
        "x_prompt": nrm(ks[0], (BATCH, SEQ, D_MODEL), 1.0),
        "x_sample": nrm(ks[1], (DEC_BATCH, DEC_SEQ, D_MODEL), 1.0),
        "state_gla": nrm(ks[2], (N_GLA, DEC_BATCH, N_HEADS, HEAD_K, HEAD_V), 0.1),
        "state_conv": nrm(ks[3], (N_CONV, DEC_BATCH, CONV_W - 1, D_INNER), 0.5),
        "norm_pre": 1.0 + nrm(ks[4], (DEPTH, D_MODEL), 0.05),
        "norm_post": 1.0 + nrm(ks[5], (DEPTH, D_MODEL), 0.05),
        "gla_w_in": nrm(ks[6], (N_GLA, D_MODEL, 2 * DK_TOT + 2 * DV_TOT), D_MODEL ** -0.5),
        "gla_w_f1": nrm(ks[7], (N_GLA, D_MODEL, GATE_RANK), D_MODEL ** -0.5),
        "gla_w_f2": nrm(ks[8], (N_GLA, GATE_RANK, DK_TOT), GATE_RANK ** -0.5),
        "gla_b_f": nrm(ks[9], (N_GLA, DK_TOT), 0.1) + 1.0,
        "gla_g_head": 1.0 + nrm(ks[10], (N_GLA, DV_TOT), 0.05),
        "gla_w_out": nrm(ks[11], (N_GLA, DV_TOT, D_MODEL), DV_TOT ** -0.5),
        "conv_w_in": nrm(ks[12], (N_CONV, D_MODEL, 4 * D_INNER), D_MODEL ** -0.5),
        "conv_w": nrm(ks[13], (N_CONV, CONV_W, D_INNER), CONV_W ** -0.5),
        "conv_w_out": nrm(ks[14], (N_CONV, D_INNER, D_MODEL), D_INNER ** -0.5),
    }


def reference(x_prompt, x_sample, state_gla, state_conv, norm_pre, norm_post,
              gla_w_in, gla_w_f1, gla_w_f2, gla_b_f, gla_g_head, gla_w_out,
              conv_w_in, conv_w, conv_w_out):
    xp, xs = x_prompt, x_sample
    s0_prompt = jnp.zeros((xp.shape[0], N_HEADS, HEAD_K, HEAD_V), xp.dtype)
    hist_prompt = jnp.zeros((xp.shape[0], CONV_W - 1, D_INNER), xp.dtype)
    gla_p, gla_s, conv_p, conv_s = [], [], [], []
    for i in range(DEPTH):
        li = i // N_MIXERS
        if i % N_MIXERS == 0:
            prm = (gla_w_in[li], gla_w_f1[li], gla_w_f2[li], gla_b_f[li], gla_g_head[li], gla_w_out[li])
            yp, sp = gla_mixer(rms_norm(xp, norm_pre[i]), s0_prompt, *prm)
            ys, ss = gla_mixer(rms_norm(xs, norm_pre[i]), state_gla[li], *prm)
            gla_p.append(sp)
            gla_s.append(ss)
        else:
            prm = (conv_w_in[li], conv_w[li], conv_w_out[li])
            yp, sp = conv_mixer(rms_norm(xp, norm_pre[i]), hist_prompt, *prm)
            ys, ss = conv_mixer(rms_norm(xs, norm_pre[i]), state_conv[li], *prm)
            conv_p.append(sp)
            conv_s.append(ss)
        xp = xp + rms_norm(yp, norm_post[i])
        xs = xs + rms_norm(ys, norm_post[i])
    return (xp, xs, jnp.stack(gla_p), jnp.stack(conv_p), jnp.stack(gla_s), jnp.stack(conv_s))
```

```python
import functools

import numpy as np
import jax
import jax.numpy as jnp
from jax import lax
from jax.experimental import pallas as pl
from jax.experimental.pallas import tpu as pltpu

D_MODEL = 1024
N_HEADS = 4
HEAD_K = 128
HEAD_V = 512
DK_TOT = N_HEADS * HEAD_K
DV_TOT = N_HEADS * HEAD_V
D_INNER = 2 * D_MODEL
GATE_NORM = 16.0
CONV_W = 3
EPS = 1e-6

SUBLANES = 8
LANES = 128
VMEM_LIMIT_BYTES = 56 * 1024 * 1024

MAX_TILE = 512
MAX_CHUNK = 64
CONV_COLS = 512
HIST_ROW = SUBLANES

F32 = jnp.float32
BF16 = jnp.bfloat16


def _dot(a, b):
    return jnp.dot(a, b, preferred_element_type=F32)


def _rms(x, g):
    return x * lax.rsqrt(jnp.mean(x * x, axis=-1, keepdims=True) + EPS) * g


def _silu(x):
    return x / (1.0 + jnp.exp(-x))


def _chunk_constants(chunk):
    nlev = int(np.log2(chunk // SUBLANES))
    assert SUBLANES << nlev == chunk
    r = np.arange(chunk)[:, None]
    u = np.arange(chunk)[None, :]
    mats = [u <= r, u > r]
    level = np.full((chunk, chunk), -1, np.int32)
    for li in range(nlev):
        half = chunk >> (li + 1)
        blk = 2 * half
        mid = (r // blk) * blk + half - 1
        upper = (r % blk) >= half
        mats.append(np.where(upper, (u > mid) & (u <= r), (u > r) & (u <= mid)))
        level[(r // blk == u // blk) & upper & ((u % blk) < half)] = li
    mats.append((u >= (r // SUBLANES) * SUBLANES) & (u <= r))
    level[(r // SUBLANES == u // SUBLANES) & (u <= r)] = nlev
    mall = np.concatenate(mats, axis=0).astype(np.float32)
    return nlev, jnp.asarray(mall, BF16), jnp.asarray(level)


def _gla_kernel(x_ref, s0_ref, gpre_ref, gpost_ref, win_ref, wf1_ref, wf2_ref, bf_ref,
                ghead_ref, wout_ref, mall_ref, level_ref,
                y_ref, sout_ref,
                s_scr, q_scr, k_scr, g_scr, v_scr, r_scr, og_scr, bl_scr,
                *, tile, chunk, nlev):
    i = pl.program_id(1)

    @pl.when(i == 0)
    def _():
        s_scr[...] = s0_ref[0]

    x = x_ref[0]
    h = _rms(x, gpre_ref[...]).astype(BF16)

    q_scr[...] = _dot(h, win_ref[:, 0:DK_TOT]) * (HEAD_K ** -0.5)
    k_scr[...] = _dot(h, win_ref[:, DK_TOT:2 * DK_TOT])
    v_scr[...] = _dot(h, win_ref[:, 2 * DK_TOT:2 * DK_TOT + DV_TOT]).astype(BF16)
    r_scr[...] = _silu(_dot(h, win_ref[:, 2 * DK_TOT + DV_TOT:])).astype(BF16)
    hf = _dot(h, wf1_ref[...]).astype(BF16)
    xg = _dot(hf, wf2_ref[...]) + bf_ref[...]
    g_scr[...] = (jnp.minimum(xg, 0.0) - jnp.log1p(jnp.exp(-jnp.abs(xg)))) * (1.0 / GATE_NORM)

    n_exp = 2 + nlev
    row = lax.broadcasted_iota(jnp.int32, (chunk, DK_TOT), 0)
    lane_d = lax.broadcasted_iota(jnp.int32, (SUBLANES, chunk), 1)
    eye = (lax.broadcasted_iota(jnp.int32, (HEAD_K, HEAD_K), 0)
           == lax.broadcasted_iota(jnp.int32, (HEAD_K, HEAD_K), 1))
    nt = (((1,), (1,)), ((), ()))
    tn = (((0,), (0,)), ((), ()))

    def chunk_body(c, carry):
        base = pl.multiple_of(c * chunk, chunk)
        rows = pl.ds(base, chunk)
        g = g_scr[rows, :]
        g_hi = g.astype(BF16)
        g_lo = (g - g_hi.astype(F32)).astype(BF16)
        mall = mall_ref[...]
        expo = _dot(mall, g_hi) + _dot(mall, g_lo)
        dec = jnp.exp(expo[:n_exp * chunk])
        bloc = expo[n_exp * chunk:]
        bl_scr[...] = bloc
        q = q_scr[rows, :]
        k = k_scr[rows, :]
        q_in = (q * dec[0:chunk]).astype(BF16)
        k_st = (k * dec[chunk:2 * chunk]).astype(BF16)
        d_last = dec[chunk - 1:chunk]
        w_lev = []
        for li in range(nlev):
            half = chunk >> (li + 1)
            upper = (row & (2 * half - 1)) >= half
            w_lev.append((jnp.where(upper, q, k)
                          * dec[(2 + li) * chunk:(3 + li) * chunk]).astype(BF16))

        a_diag = [[] for _ in range(N_HEADS)]
        for bi in range(chunk // SUBLANES):
            r0 = bi * SUBLANES
            qb = q[r0:r0 + SUBLANES]
            bb = bloc[r0:r0 + SUBLANES]
            blk = [jnp.zeros((SUBLANES, chunk), F32) for _ in range(N_HEADS)]
            for j in range(SUBLANES):
                k_row = k_scr[pl.ds(base + r0 + j, 1), :]
                b_row = bl_scr[pl.ds(r0 + j, 1), :]
                p = qb * (k_row * jnp.exp(bb - b_row))
                for hd in range(N_HEADS):
                    rs = jnp.sum(p[:, hd * HEAD_K:(hd + 1) * HEAD_K], axis=1, keepdims=True)
                    blk[hd] = jnp.where(lane_d == r0 + j, rs, blk[hd])
            for hd in range(N_HEADS):
                a_diag[hd].append(blk[hd])

        level = level_ref[...]
        for hd in range(N_HEADS):
            ks = slice(hd * HEAD_K, (hd + 1) * HEAD_K)
            vs = slice(hd * HEAD_V, (hd + 1) * HEAD_V)
            att = jnp.zeros((chunk, chunk), F32)
            for li in range(nlev):
                w = w_lev[li][:, ks]
                att = jnp.where(level == li,
                                lax.dot_general(w, w, nt, preferred_element_type=F32), att)
            att = jnp.where(level == nlev, jnp.concatenate(a_diag[hd], axis=0), att)
            v = v_scr[rows, vs]
            s = s_scr[hd]
            o = _dot(q_in[:, ks], s.astype(BF16)) + _dot(att.astype(BF16), v)
            on = _rms(o, ghead_ref[:, vs])
            og_scr[rows, vs] = (on * r_scr[rows, vs].astype(F32)).astype(BF16)
            d_col = jnp.sum(jnp.where(eye, d_last[:, ks], 0.0), axis=1, keepdims=True)
            kv = lax.dot_general(k_st[:, ks], v, tn, preferred_element_type=F32)
            s_scr[hd] = d_col * s + kv
        return carry

    lax.fori_loop(0, tile // chunk, chunk_body, 0)

    y = _dot(og_scr[...], wout_ref[...])
    y_ref[0] = x + _rms(y, gpost_ref[...])

    @pl.when(i == pl.num_programs(1) - 1)
    def _():
        sout_ref[0] = s_scr[...]


def _conv_kernel(x_ref, hist_ref, gpre_ref, gpost_ref, win_ref, cw_ref, wout_ref,
                 y_ref, hout_ref,
                 xc_scr,
                 *, tile):
    i = pl.program_id(1)

    @pl.when(i == 0)
    def _():
        xc_scr[0:HIST_ROW, :] = hist_ref[0]

    x = x_ref[0]
    h = _rms(x, gpre_ref[...]).astype(BF16)
    acc = jnp.zeros((tile, D_MODEL), F32)
    for cb in range(D_INNER // CONV_COLS):
        cols = slice(cb * CONV_COLS, (cb + 1) * CONV_COLS)

        def proj(part):
            lo = part * D_INNER + cb * CONV_COLS
            return _dot(h, win_ref[:, lo:lo + CONV_COLS])

        xc = proj(1) * proj(2)
        xc_scr[HIST_ROW:HIST_ROW + tile, cols] = xc
        x2 = xc_scr[HIST_ROW - 2:HIST_ROW - 2 + tile, cols]
        x1 = xc_scr[HIST_ROW - 1:HIST_ROW - 1 + tile, cols]
        conv = x2 * cw_ref[0:1, cols] + x1 * cw_ref[1:2, cols] + xc * cw_ref[2:3, cols]
        yb = (proj(0) * conv * _silu(proj(3))).astype(BF16)
        acc = acc + _dot(yb, wout_ref[cols, :])
        xc_scr[0:HIST_ROW, cols] = xc[tile - HIST_ROW:tile]
    y_ref[0] = x + _rms(acc, gpost_ref[...])

    @pl.when(i == pl.num_programs(1) - 1)
    def _():
        hout_ref[0] = xc_scr[0:HIST_ROW, :]


def _tile_sizes(t_len):
    tile = min(t_len, MAX_TILE)
    chunk = min(tile, MAX_CHUNK)
    assert t_len % tile == 0 and tile % chunk == 0 and chunk % SUBLANES == 0
    return tile, chunk


def _const_spec(shape):
    return pl.BlockSpec(shape, lambda b, i: (0,) * len(shape))


def _gla_layer(x, s0, gpre, gpost, win, wf1, wf2, bf, ghead, wout):
    bsz, t_len, _ = x.shape
    tile, chunk = _tile_sizes(t_len)
    nlev, mall, level = _chunk_constants(chunk)
    n_rows = mall.shape[0]
    kern = functools.partial(_gla_kernel, tile=tile, chunk=chunk, nlev=nlev)
    tok_spec = pl.BlockSpec((1, tile, D_MODEL), lambda b, i: (b, i, 0))
    st_spec = pl.BlockSpec((1, N_HEADS, HEAD_K, HEAD_V), lambda b, i: (b, 0, 0, 0))
    return pl.pallas_call(
        kern,
        grid=(bsz, t_len // tile),
        in_specs=[
            tok_spec, st_spec,
            _const_spec((1, D_MODEL)), _const_spec((1, D_MODEL)),
            _const_spec((D_MODEL, 2 * DK_TOT + 2 * DV_TOT)),
            _const_spec((D_MODEL, LANES)), _const_spec((LANES, DK_TOT)),
            _const_spec((1, DK_TOT)), _const_spec((1, DV_TOT)),
            _const_spec((DV_TOT, D_MODEL)),
            _const_spec((n_rows, chunk)), _const_spec((chunk, chunk)),
        ],
        out_specs=[tok_spec, st_spec],
        out_shape=[jax.ShapeDtypeStruct(x.shape, F32),
                   jax.ShapeDtypeStruct(s0.shape, F32)],
        scratch_shapes=[
            pltpu.VMEM((N_HEADS, HEAD_K, HEAD_V), F32),
            pltpu.VMEM((tile, DK_TOT), F32),
            pltpu.VMEM((tile, DK_TOT), F32),
            pltpu.VMEM((tile, DK_TOT), F32),
            pltpu.VMEM((tile, DV_TOT), BF16),
            pltpu.VMEM((tile, DV_TOT), BF16),
            pltpu.VMEM((tile, DV_TOT), BF16),
            pltpu.VMEM((chunk, DK_TOT), F32),
        ],
        compiler_params=pltpu.CompilerParams(
            dimension_semantics=("arbitrary", "arbitrary"),
            vmem_limit_bytes=VMEM_LIMIT_BYTES),
        name="gla_layer",
    )(x, s0, gpre, gpost, win, wf1, wf2, bf, ghead, wout, mall, level)


def _conv_layer(x, hist, gpre, gpost, win, cw, wout):
    bsz, t_len, _ = x.shape
    tile, _ = _tile_sizes(t_len)
    assert tile >= HIST_ROW
    hist8 = jnp.pad(hist, ((0, 0), (HIST_ROW - (CONV_W - 1), 0), (0, 0)))
    kern = functools.partial(_conv_kernel, tile=tile)
    tok_spec = pl.BlockSpec((1, tile, D_MODEL), lambda b, i: (b, i, 0))
    hist_spec = pl.BlockSpec((1, HIST_ROW, D_INNER), lambda b, i: (b, 0, 0))
    y, hout = pl.pallas_call(
        kern,
        grid=(bsz, t_len // tile),
        in_specs=[
            tok_spec, hist_spec,
            _const_spec((1, D_MODEL)), _const_spec((1, D_MODEL)),
            _const_spec((D_MODEL, 4 * D_INNER)),
            _const_spec((CONV_W, D_INNER)),
            _const_spec((D_INNER, D_MODEL)),
        ],
        out_specs=[tok_spec, hist_spec],
        out_shape=[jax.ShapeDtypeStruct(x.shape, F32),
                   jax.ShapeDtypeStruct((bsz, HIST_ROW, D_INNER), F32)],
        scratch_shapes=[pltpu.VMEM((HIST_ROW + tile, D_INNER), F32)],
        compiler_params=pltpu.CompilerParams(
            dimension_semantics=("arbitrary", "arbitrary"),
            vmem_limit_bytes=VMEM_LIMIT_BYTES),
        name="conv_layer",
    )(x, hist8, gpre, gpost, win, cw, wout)
    return y, hout[:, HIST_ROW - (CONV_W - 1):]


def kernel(x_prompt, x_sample, state_gla, state_conv, norm_pre, norm_post, gla_w_in, gla_w_f1, gla_w_f2, gla_b_f, gla_g_head, gla_w_out, conv_w_in, conv_w, conv_w_out):
    depth = norm_pre.shape[0]
    xp, xs = x_prompt, x_sample
    s0_prompt = jnp.zeros((xp.shape[0], N_HEADS, HEAD_K, HEAD_V), F32)
    hist_prompt = jnp.zeros((xp.shape[0], CONV_W - 1, D_INNER), F32)
    gla_p, gla_s, conv_p, conv_s = [], [], [], []
    for layer in range(depth):
        li = layer // 2
        gpre = norm_pre[layer][None]
        gpost = norm_post[layer][None]
        if layer % 2 == 0:
            rank = gla_w_f1.shape[-1]
            prm = (gpre, gpost,
                   gla_w_in[li].astype(BF16),
                   jnp.pad(gla_w_f1[li], ((0, 0), (0, LANES - rank))).astype(BF16),
                   jnp.pad(gla_w_f2[li], ((0, LANES - rank), (0, 0))).astype(BF16),
                   gla_b_f[li][None], gla_g_head[li][None],
                   gla_w_out[li].astype(BF16))
            xp, sp = _gla_layer(xp, s0_prompt, *prm)
            xs, ss = _gla_layer(xs, state_gla[li], *prm)
            gla_p.append(sp)
            gla_s.append(ss)
        else:
            prm = (gpre, gpost, conv_w_in[li].astype(BF16), conv_w[li],
                   conv_w_out[li].astype(BF16))
            xp, sp = _conv_layer(xp, hist_prompt, *prm)
            xs, ss = _conv_layer(xs, state_conv[li], *prm)
            conv_p.append(sp)
            conv_s.append(ss)
    return (xp, xs, jnp.stack(gla_p), jnp.stack(conv_p), jnp.stack(gla_s), jnp.stack(conv_s))
```

```python
import functools

import numpy as np
import jax
import jax.numpy as jnp
from jax import lax
from jax.experimental import pallas as pl
from jax.experimental.pallas import tpu as pltpu

D_MODEL = 1024
N_HEADS = 4
HEAD_K = 128
HEAD_V = 512
DK_TOT = N_HEADS * HEAD_K
DV_TOT = N_HEADS * HEAD_V
D_INNER = 2 * D_MODEL
GATE_NORM = 16.0
CONV_W = 3
EPS = 1e-6
LOG2_E = 1.4426950408889634

SUBLANES = 8
LANES = 128
MXU_COLS = 256
VMEM_LIMIT_BYTES = 56 * 1024 * 1024

MAX_TILE = 512
MAX_CHUNK = 64
ROW_BLOCK = 128
DIAG_BLOCKS_PER_TICK = 2
CONV_COLS = 512
HIST_ROW = SUBLANES

F32 = jnp.float32
BF16 = jnp.bfloat16


def _dot(a, b):
    return jnp.dot(a, b, preferred_element_type=F32)


def _rms(x, g):
    return x * lax.rsqrt(jnp.mean(x * x, axis=-1, keepdims=True) + EPS) * g


def _silu(x):
    return x / (1.0 + jnp.exp(-x))


def _row_bcast(ref, lead, r):
    return ref[lead, pl.ds(r, SUBLANES, stride=0), :]


def _chunk_constants(chunk):
    nlev = int(np.log2(chunk // SUBLANES))
    assert SUBLANES << nlev == chunk
    r = np.arange(chunk)[:, None]
    u = np.arange(chunk)[None, :]
    mats = [u <= r, u > r]
    level = np.full((chunk, chunk), -1, np.int32)
    for li in range(nlev):
        half = chunk >> (li + 1)
        blk = 2 * half
        mid = (r // blk) * blk + half - 1
        upper = (r % blk) >= half
        mats.append(np.where(upper, (u > mid) & (u <= r), (u > r) & (u <= mid)))
        level[(r // blk == u // blk) & upper & ((u % blk) < half)] = li
    mats.append((u >= (r // SUBLANES) * SUBLANES) & (u <= r))
    level[(r // SUBLANES == u // SUBLANES) & (u <= r)] = nlev
    mall = np.concatenate(mats, axis=0).astype(np.float32)
    mall = np.concatenate([mall, mall], axis=1)
    lane = np.arange(LANES)[:, None]
    spread = ((lane < SUBLANES) & (u % SUBLANES == lane)).astype(np.float32)
    return nlev, jnp.asarray(mall, BF16), jnp.asarray(level), jnp.asarray(spread, BF16)


class _WorkQueue:
    def __init__(self, items, n_ticks):
        self.items = list(items)
        self.total = len(self.items)
        self.n_ticks = max(n_ticks, 1)
        self.done = 0
        self.ticks = 0

    def tick(self):
        self.ticks += 1
        target = min(self.total, -(-self.total * self.ticks // self.n_ticks))
        while self.done < target:
            self.items[self.done]()
            self.done += 1

    def flush(self):
        while self.done < self.total:
            self.items[self.done]()
            self.done += 1


def _gla_kernel(x_ref, xn_ref, s0_ref, gpre_ref, gpost_ref, win_ref, wf1_ref, wf2_ref, bf_ref,
                ghead_ref, wout_ref, mall_ref, level_ref, spread_ref,
                y_ref, sout_ref,
                s_scr, h_scr, q_scr, k_scr, g_scr, v_scr, r_scr, og_scr, bl_scr, y_scr,
                *, tile, chunk, nlev, pipelined):
    i = pl.program_id(1)
    n_chunks = tile // chunk
    n_exp = 2 + nlev
    row = lax.broadcasted_iota(jnp.int32, (chunk, HEAD_K), 0)
    lane = lax.broadcasted_iota(jnp.int32, (SUBLANES, LANES), 1)
    eye = (lax.broadcasted_iota(jnp.int32, (HEAD_K, HEAD_K), 0)
           == lax.broadcasted_iota(jnp.int32, (HEAD_K, HEAD_K), 1))
    nt = (((1,), (1,)), ((), ()))
    tn = (((0,), (0,)), ((), ()))

    def proj_items(src_ref, src_rows, rows):
        def norm():
            h_scr[rows, :] = _rms(src_ref[0, src_rows, :], gpre_ref[...]).astype(BF16)

        def cols(lo):
            return _dot(h_scr[rows, :], win_ref[lo // MXU_COLS])

        def q_item(pair):
            def f():
                res = cols(pair * MXU_COLS) * (HEAD_K ** -0.5)
                q_scr[2 * pair, rows, :] = res[:, :HEAD_K]
                q_scr[2 * pair + 1, rows, :] = res[:, HEAD_K:]
            return f

        def k_item(pair):
            def f():
                res = cols(DK_TOT + pair * MXU_COLS)
                k_scr[2 * pair, rows, :] = res[:, :HEAD_K]
                k_scr[2 * pair + 1, rows, :] = res[:, HEAD_K:]
            return f

        def v_item(n):
            def f():
                v_scr[rows, n * MXU_COLS:(n + 1) * MXU_COLS] = cols(
                    2 * DK_TOT + n * MXU_COLS).astype(BF16)
            return f

        def r_item(n):
            def f():
                r_scr[rows, n * MXU_COLS:(n + 1) * MXU_COLS] = _silu(cols(
                    2 * DK_TOT + DV_TOT + n * MXU_COLS)).astype(BF16)
            return f

        def gate():
            hf = _dot(h_scr[rows, :], wf1_ref[...]).astype(BF16)
            xg = _dot(hf, wf2_ref[...]) + bf_ref[...]
            g_scr[rows, :] = ((jnp.minimum(xg, 0.0) - jnp.log1p(jnp.exp(-jnp.abs(xg))))
                              * (LOG2_E / GATE_NORM))

        items = [norm, gate]
        items += [q_item(p) for p in range(DK_TOT // MXU_COLS)]
        items += [k_item(p) for p in range(DK_TOT // MXU_COLS)]
        for n in range(DV_TOT // MXU_COLS):
            items += [v_item(n), r_item(n)]
        return items

    def out_items(rows):
        def o_item(n):
            def f():
                y_scr[rows, n * MXU_COLS:(n + 1) * MXU_COLS] = _dot(og_scr[rows, :], wout_ref[n])
            return f

        def post():
            y_ref[0, rows, :] = x_ref[0, rows, :] + _rms(y_scr[rows, :], gpost_ref[...])

        return [o_item(n) for n in range(D_MODEL // MXU_COLS)] + [post]

    def run_chunk(c, queue):
        base = c * chunk
        rows = slice(base, base + chunk)
        g = g_scr[rows, :]
        g_hi = g.astype(BF16)
        g_lo = (g - g_hi.astype(F32)).astype(BF16)
        expo = _dot(mall_ref[...], jnp.concatenate([g_hi, g_lo], axis=0))
        dec = jnp.exp2(expo[:n_exp * chunk])
        level = level_ref[...]
        queue.tick()
        for hd in range(N_HEADS):
            ks = slice(hd * HEAD_K, (hd + 1) * HEAD_K)
            vs = slice(hd * HEAD_V, (hd + 1) * HEAD_V)
            q = q_scr[hd, rows, :]
            k = k_scr[hd, rows, :]
            bloc = expo[n_exp * chunk:, ks]
            bl_scr[hd] = bloc
            q_in = (q * dec[0:chunk, ks]).astype(BF16)
            k_st = (k * dec[chunk:2 * chunk, ks]).astype(BF16)
            d_last = dec[chunk - 1:chunk, ks]

            att = jnp.zeros((chunk, chunk), F32)
            for li in range(nlev):
                half = chunk >> (li + 1)
                upper = (row & (2 * half - 1)) >= half
                w = (jnp.where(upper, q, k) * dec[(2 + li) * chunk:(3 + li) * chunk, ks]).astype(BF16)
                att = jnp.where(level == li,
                                lax.dot_general(w, w, nt, preferred_element_type=F32), att)
            queue.tick()

            blocks = []
            for bi in range(chunk // SUBLANES):
                r0 = bi * SUBLANES
                qb = q[r0:r0 + SUBLANES]
                bb = bloc[r0:r0 + SUBLANES]
                blk = jnp.zeros((SUBLANES, LANES), F32)
                for j in range(SUBLANES):
                    k_row = _row_bcast(k_scr, hd, base + r0 + j)
                    b_row = _row_bcast(bl_scr, hd, r0 + j)
                    p = qb * (k_row * jnp.exp2(bb - b_row))
                    blk = jnp.where(lane == j, jnp.sum(p, axis=1, keepdims=True), blk)
                blocks.append(blk)
                if bi % DIAG_BLOCKS_PER_TICK == DIAG_BLOCKS_PER_TICK - 1:
                    queue.tick()
            a_diag = _dot(jnp.concatenate(blocks, axis=0).astype(BF16), spread_ref[...])
            att = jnp.where(level == nlev, a_diag, att)
            queue.tick()

            v = v_scr[rows, vs]
            s = s_scr[hd]
            o = _dot(q_in, s.astype(BF16)) + _dot(att.astype(BF16), v)
            on = _rms(o, ghead_ref[:, vs])
            og_scr[rows, vs] = (on * r_scr[rows, vs].astype(F32)).astype(BF16)
            d_col = jnp.sum(jnp.where(eye, d_last, 0.0), axis=1, keepdims=True)
            kv = lax.dot_general(k_st, v, tn, preferred_element_type=F32)
            s_scr[hd] = d_col * s + kv

    def run_phase(chunks, items):
        ticks_per_head = 2 + (chunk // SUBLANES) // DIAG_BLOCKS_PER_TICK
        queue = _WorkQueue(items, len(chunks) * (1 + N_HEADS * ticks_per_head))
        for c in chunks:
            run_chunk(c, queue)
        queue.flush()

    def blocks(lo, hi, step):
        return [slice(r, r + step) for r in range(lo, hi, step)]

    if pipelined:
        half = tile // 2
        first, second = blocks(0, half, ROW_BLOCK), blocks(half, tile, ROW_BLOCK)

        @pl.when(i == 0)
        def _():
            s_scr[...] = s0_ref[0]
            for rows in first:
                for item in proj_items(x_ref, rows, rows):
                    item()

        run_phase(range(0, n_chunks // 2),
                  [it for rows in second for it in proj_items(x_ref, rows, rows)])
        run_phase(range(n_chunks // 2, n_chunks),
                  [it for rows in first for it in proj_items(xn_ref, rows, rows)]
                  + [it for rows in first for it in out_items(rows)])
        for rows in second:
            for item in out_items(rows):
                item()
    else:
        @pl.when(i == 0)
        def _():
            s_scr[...] = s0_ref[0]

        whole = slice(0, tile)
        for item in proj_items(x_ref, whole, whole):
            item()
        run_phase(range(n_chunks), [])
        for item in out_items(whole):
            item()

    @pl.when(i == pl.num_programs(1) - 1)
    def _():
        sout_ref[0] = s_scr[...]


def _conv_kernel(x_ref, hist_ref, gpre_ref, gpost_ref, win_ref, cw_ref, wout_ref,
                 y_ref, hout_ref,
                 xc_scr,
                 *, tile):
    i = pl.program_id(1)

    @pl.when(i == 0)
    def _():
        xc_scr[0:HIST_ROW, :] = hist_ref[0]

    x = x_ref[0]
    h = _rms(x, gpre_ref[...]).astype(BF16)
    n_out = D_MODEL // MXU_COLS
    acc = [jnp.zeros((tile, MXU_COLS), F32) for _ in range(n_out)]
    n_cb = D_INNER // CONV_COLS
    for cb in range(n_cb):
        cols = slice(cb * CONV_COLS, (cb + 1) * CONV_COLS)

        def proj(part):
            return _dot(h, win_ref[part * n_cb + cb])

        xc = proj(1) * proj(2)
        xc_scr[HIST_ROW:HIST_ROW + tile, cols] = xc
        x2 = xc_scr[HIST_ROW - 2:HIST_ROW - 2 + tile, cols]
        x1 = xc_scr[HIST_ROW - 1:HIST_ROW - 1 + tile, cols]
        conv = x2 * cw_ref[0:1, cols] + x1 * cw_ref[1:2, cols] + xc * cw_ref[2:3, cols]
        yb = (proj(0) * conv * _silu(proj(3))).astype(BF16)
        for n in range(n_out):
            acc[n] = acc[n] + _dot(yb, wout_ref[n, cols, :])
        xc_scr[0:HIST_ROW, cols] = xc[tile - HIST_ROW:tile]
    y_ref[0] = x + _rms(jnp.concatenate(acc, axis=1), gpost_ref[...])

    @pl.when(i == pl.num_programs(1) - 1)
    def _():
        hout_ref[0] = xc_scr[0:HIST_ROW, :]


def _tile_sizes(t_len):
    tile = min(t_len, MAX_TILE)
    chunk = min(tile, MAX_CHUNK)
    assert t_len % tile == 0 and tile % chunk == 0 and chunk % SUBLANES == 0
    return tile, chunk


def _const_spec(shape):
    return pl.BlockSpec(shape, lambda b, i: (0,) * len(shape), pipeline_mode=pl.Buffered(1))


def _col_pieces(w, width):
    k_dim, n_dim = w.shape
    return w.astype(BF16).reshape(k_dim, n_dim // width, width).transpose(1, 0, 2)


def _gla_layer(x, s0, gpre, gpost, win, wf1, wf2, bf, ghead, wout):
    bsz, t_len, _ = x.shape
    tile, chunk = _tile_sizes(t_len)
    nlev, mall, level, spread = _chunk_constants(chunk)
    n_tiles = t_len // tile
    n_chunks = tile // chunk
    half = tile // 2
    pipelined = n_chunks % 2 == 0 and half % ROW_BLOCK == 0
    kern = functools.partial(_gla_kernel, tile=tile, chunk=chunk, nlev=nlev, pipelined=pipelined)
    tok_spec = pl.BlockSpec((1, tile, D_MODEL), lambda b, i: (b, i, 0))
    next_spec = pl.BlockSpec((1, half, D_MODEL),
                             lambda b, i: (b, jnp.minimum(2 * i + 2, 2 * n_tiles - 1), 0))
    st_spec = pl.BlockSpec((1, N_HEADS, HEAD_K, HEAD_V), lambda b, i: (b, 0, 0, 0))
    return pl.pallas_call(
        kern,
        grid=(bsz, n_tiles),
        in_specs=[
            tok_spec, next_spec, st_spec,
            _const_spec((1, D_MODEL)), _const_spec((1, D_MODEL)),
            _const_spec(win.shape),
            _const_spec((D_MODEL, LANES)), _const_spec((LANES, DK_TOT)),
            _const_spec((1, DK_TOT)), _const_spec((1, DV_TOT)),
            _const_spec(wout.shape),
            _const_spec(mall.shape), _const_spec((chunk, chunk)),
            _const_spec((LANES, chunk)),
        ],
        out_specs=[tok_spec, st_spec],
        out_shape=[jax.ShapeDtypeStruct(x.shape, F32),
                   jax.ShapeDtypeStruct(s0.shape, F32)],
        scratch_shapes=[
            pltpu.VMEM((N_HEADS, HEAD_K, HEAD_V), F32),
            pltpu.VMEM((tile, D_MODEL), BF16),
            pltpu.VMEM((N_HEADS, tile, HEAD_K), F32),
            pltpu.VMEM((N_HEADS, tile, HEAD_K), F32),
            pltpu.VMEM((tile, DK_TOT), F32),
            pltpu.VMEM((tile, DV_TOT), BF16),
            pltpu.VMEM((tile, DV_TOT), BF16),
            pltpu.VMEM((tile, DV_TOT), BF16),
            pltpu.VMEM((N_HEADS, chunk, HEAD_K), F32),
            pltpu.VMEM((tile, D_MODEL), F32),
        ],
        compiler_params=pltpu.CompilerParams(
            dimension_semantics=("arbitrary", "arbitrary"),
            vmem_limit_bytes=VMEM_LIMIT_BYTES),
        name="gla_layer",
    )(x, x, s0, gpre, gpost, win, wf1, wf2, bf, ghead, wout, mall, level, spread)


def _conv_layer(x, hist, gpre, gpost, win, cw, wout):
    bsz, t_len, _ = x.shape
    tile, _ = _tile_sizes(t_len)
    assert tile >= HIST_ROW
    hist8 = jnp.pad(hist, ((0, 0), (HIST_ROW - (CONV_W - 1), 0), (0, 0)))
    kern = functools.partial(_conv_kernel, tile=tile)
    tok_spec = pl.BlockSpec((1, tile, D_MODEL), lambda b, i: (b, i, 0))
    hist_spec = pl.BlockSpec((1, HIST_ROW, D_INNER), lambda b, i: (b, 0, 0))
    y, hout = pl.pallas_call(
        kern,
        grid=(bsz, t_len // tile),
        in_specs=[
            tok_spec, hist_spec,
            _const_spec((1, D_MODEL)), _const_spec((1, D_MODEL)),
            _const_spec(win.shape),
            _const_spec((CONV_W, D_INNER)),
            _const_spec(wout.shape),
        ],
        out_specs=[tok_spec, hist_spec],
        out_shape=[jax.ShapeDtypeStruct(x.shape, F32),
                   jax.ShapeDtypeStruct((bsz, HIST_ROW, D_INNER), F32)],
        scratch_shapes=[pltpu.VMEM((HIST_ROW + tile, D_INNER), F32)],
        compiler_params=pltpu.CompilerParams(
            dimension_semantics=("arbitrary", "arbitrary"),
            vmem_limit_bytes=VMEM_LIMIT_BYTES),
        name="conv_layer",
    )(x, hist8, gpre, gpost, win, cw, wout)
    return y, hout[:, HIST_ROW - (CONV_W - 1):]


def kernel(x_prompt, x_sample, state_gla, state_conv, norm_pre, norm_post, gla_w_in, gla_w_f1, gla_w_f2, gla_b_f, gla_g_head, gla_w_out, conv_w_in, conv_w, conv_w_out):
    depth = norm_pre.shape[0]
    xp, xs = x_prompt, x_sample
    s0_prompt = jnp.zeros((xp.shape[0], N_HEADS, HEAD_K, HEAD_V), F32)
    hist_prompt = jnp.zeros((xp.shape[0], CONV_W - 1, D_INNER), F32)
    gla_p, gla_s, conv_p, conv_s = [], [], [], []
    for layer in range(depth):
        li = layer // 2
        gpre = norm_pre[layer][None]
        gpost = norm_post[layer][None]
        if layer % 2 == 0:
            rank = gla_w_f1.shape[-1]
            prm = (gpre, gpost,
                   _col_pieces(gla_w_in[li], MXU_COLS),
                   jnp.pad(gla_w_f1[li], ((0, 0), (0, LANES - rank))).astype(BF16),
                   jnp.pad(gla_w_f2[li], ((0, LANES - rank), (0, 0))).astype(BF16),
                   gla_b_f[li][None], gla_g_head[li][None],
                   _col_pieces(gla_w_out[li], MXU_COLS))
            xp, sp = _gla_layer(xp, s0_prompt, *prm)
            xs, ss = _gla_layer(xs, state_gla[li], *prm)
            gla_p.append(sp)
            gla_s.append(ss)
        else:
            prm = (gpre, gpost, _col_pieces(conv_w_in[li], CONV_COLS), conv_w[li],
                   _col_pieces(conv_w_out[li], MXU_COLS))
            xp, sp = _conv_layer(xp, hist_prompt, *prm)
            xs, ss = _conv_layer(xs, state_conv[li], *prm)
            conv_p.append(sp)
            conv_s.append(ss)
    return (xp, xs, jnp.stack(gla_p), jnp.stack(conv_p), jnp.stack(gla_s), jnp.stack(conv_s))
```

```python
import functools

import numpy as np
import jax
import jax.numpy as jnp
from jax import lax
from jax.experimental import pallas as pl
from jax.experimental.pallas import tpu as pltpu

D_MODEL = 1024
N_HEADS = 4
HEAD_K = 128
HEAD_V = 512
DK_TOT = N_HEADS * HEAD_K
DV_TOT = N_HEADS * HEAD_V
D_INNER = 2 * D_MODEL
GATE_NORM = 16.0
CONV_W = 3
EPS = 1e-6
LOG2_E = 1.4426950408889634

SUBLANES = 8
LANES = 128
MXU_COLS = 256
VMEM_LIMIT_BYTES = 56 * 1024 * 1024

MAX_TILE = 512
MAX_CHUNK = 128
ROW_BLOCK = 128
DIAG_BLOCKS_PER_TICK = 2
CONV_COLS = 512
HIST_ROW = SUBLANES

F32 = jnp.float32
BF16 = jnp.bfloat16


def _dot(a, b):
    return jnp.dot(a, b, preferred_element_type=F32)


def _rms(x, g):
    return x * lax.rsqrt(jnp.mean(x * x, axis=-1, keepdims=True) + EPS) * g


def _silu(x):
    return x / (1.0 + jnp.exp(-x))


def _row_bcast(ref, lead, r):
    return ref[lead, pl.ds(r, SUBLANES, stride=0), :]


def _chunk_constants(chunk):
    nlev = int(np.log2(chunk // SUBLANES))
    assert SUBLANES << nlev == chunk
    r = np.arange(chunk)[:, None]
    u = np.arange(chunk)[None, :]
    mats = [u <= r, u > r]
    level = np.full((chunk, chunk), -1, np.int32)
    for li in range(nlev):
        half = chunk >> (li + 1)
        blk = 2 * half
        mid = (r // blk) * blk + half - 1
        upper = (r % blk) >= half
        mats.append(np.where(upper, (u > mid) & (u <= r), (u > r) & (u <= mid)))
        level[(r // blk == u // blk) & upper & ((u % blk) < half)] = li
    mats.append((u >= (r // SUBLANES) * SUBLANES) & (u <= r))
    level[(r // SUBLANES == u // SUBLANES) & (u <= r)] = nlev
    mall = np.concatenate(mats, axis=0).astype(np.float32)
    mall = np.concatenate([mall, mall], axis=1)
    lane = np.arange(LANES)[:, None]
    spread = ((lane < SUBLANES) & (u % SUBLANES == lane)).astype(np.float32)
    return nlev, jnp.asarray(mall, BF16), jnp.asarray(level), jnp.asarray(spread, BF16)


class _WorkQueue:
    def __init__(self, items, n_ticks):
        self.items = list(items)
        self.total = len(self.items)
        self.n_ticks = max(n_ticks, 1)
        self.done = 0
        self.ticks = 0

    def tick(self):
        self.ticks += 1
        target = min(self.total, -(-self.total * self.ticks // self.n_ticks))
        while self.done < target:
            self.items[self.done]()
            self.done += 1

    def flush(self):
        while self.done < self.total:
            self.items[self.done]()
            self.done += 1


def _gla_kernel(x_ref, xn_ref, s0_ref, gpre_ref, gpost_ref, win_ref, wf1_ref, wf2_ref, bf_ref,
                ghead_ref, wout_ref, mall_ref, level_ref, spread_ref,
                y_ref, sout_ref,
                s_scr, h_scr, q_scr, k_scr, g_scr, v_scr, r_scr, og_scr, bl_scr, y_scr,
                *, tile, chunk, nlev, pipelined):
    i = pl.program_id(1)
    n_chunks = tile // chunk
    n_exp = 2 + nlev
    row = lax.broadcasted_iota(jnp.int32, (chunk, HEAD_K), 0)
    lane = lax.broadcasted_iota(jnp.int32, (SUBLANES, LANES), 1)
    eye = (lax.broadcasted_iota(jnp.int32, (HEAD_K, HEAD_K), 0)
           == lax.broadcasted_iota(jnp.int32, (HEAD_K, HEAD_K), 1))
    nt = (((1,), (1,)), ((), ()))
    tn = (((0,), (0,)), ((), ()))

    def proj_items(src_ref, src_rows, rows):
        def norm():
            h_scr[rows, :] = _rms(src_ref[0, src_rows, :], gpre_ref[...]).astype(BF16)

        def cols(lo):
            return _dot(h_scr[rows, :], win_ref[lo // MXU_COLS])

        def q_item(pair):
            def f():
                res = cols(pair * MXU_COLS) * (HEAD_K ** -0.5)
                q_scr[2 * pair, rows, :] = res[:, :HEAD_K]
                q_scr[2 * pair + 1, rows, :] = res[:, HEAD_K:]
            return f

        def k_item(pair):
            def f():
                res = cols(DK_TOT + pair * MXU_COLS)
                k_scr[2 * pair, rows, :] = res[:, :HEAD_K]
                k_scr[2 * pair + 1, rows, :] = res[:, HEAD_K:]
            return f

        def v_item(n):
            def f():
                v_scr[rows, n * MXU_COLS:(n + 1) * MXU_COLS] = cols(
                    2 * DK_TOT + n * MXU_COLS).astype(BF16)
            return f

        def r_item(n):
            def f():
                r_scr[rows, n * MXU_COLS:(n + 1) * MXU_COLS] = _silu(cols(
                    2 * DK_TOT + DV_TOT + n * MXU_COLS)).astype(BF16)
            return f

        def gate():
            hf = _dot(h_scr[rows, :], wf1_ref[...]).astype(BF16)
            xg = _dot(hf, wf2_ref[...]) + bf_ref[...]
            g_scr[rows, :] = ((jnp.minimum(xg, 0.0) - jnp.log1p(jnp.exp(-jnp.abs(xg))))
                              * (LOG2_E / GATE_NORM))

        items = [norm, gate]
        items += [q_item(p) for p in range(DK_TOT // MXU_COLS)]
        items += [k_item(p) for p in range(DK_TOT // MXU_COLS)]
        for n in range(DV_TOT // MXU_COLS):
            items += [v_item(n), r_item(n)]
        return items

    def out_items(rows):
        def o_item(n):
            def f():
                y_scr[rows, n * MXU_COLS:(n + 1) * MXU_COLS] = _dot(og_scr[rows, :], wout_ref[n])
            return f

        def post():
            y_ref[0, rows, :] = x_ref[0, rows, :] + _rms(y_scr[rows, :], gpost_ref[...])

        return [o_item(n) for n in range(D_MODEL // MXU_COLS)] + [post]

    def run_chunk(c, queue):
        base = c * chunk
        rows = slice(base, base + chunk)
        g = g_scr[rows, :]
        g_hi = g.astype(BF16)
        g_lo = (g - g_hi.astype(F32)).astype(BF16)
        expo = _dot(mall_ref[...], jnp.concatenate([g_hi, g_lo], axis=0))
        dec = jnp.exp2(expo[:n_exp * chunk])
        level = level_ref[...]
        queue.tick()
        for hd in range(N_HEADS):
            ks = slice(hd * HEAD_K, (hd + 1) * HEAD_K)
            vs = slice(hd * HEAD_V, (hd + 1) * HEAD_V)
            q = q_scr[hd, rows, :]
            k = k_scr[hd, rows, :]
            bloc = expo[n_exp * chunk:, ks]
            bl_scr[hd] = bloc
            q_in = (q * dec[0:chunk, ks]).astype(BF16)
            k_st = (k * dec[chunk:2 * chunk, ks]).astype(BF16)
            d_last = dec[chunk - 1:chunk, ks]

            att = jnp.zeros((chunk, chunk), F32)
            for li in range(nlev):
                half = chunk >> (li + 1)
                upper = (row & (2 * half - 1)) >= half
                w = (jnp.where(upper, q, k) * dec[(2 + li) * chunk:(3 + li) * chunk, ks]).astype(BF16)
                att = jnp.where(level == li,
                                lax.dot_general(w, w, nt, preferred_element_type=F32), att)
            queue.tick()

            blocks = []
            for bi in range(chunk // SUBLANES):
                r0 = bi * SUBLANES
                qb = q[r0:r0 + SUBLANES]
                bb = bloc[r0:r0 + SUBLANES]
                blk = jnp.zeros((SUBLANES, LANES), F32)
                for j in range(SUBLANES):
                    k_row = _row_bcast(k_scr, hd, base + r0 + j)
                    b_row = _row_bcast(bl_scr, hd, r0 + j)
                    p = qb * (k_row * jnp.exp2(bb - b_row))
                    blk = jnp.where(lane == j, jnp.sum(p, axis=1, keepdims=True), blk)
                blocks.append(blk)
                if bi % DIAG_BLOCKS_PER_TICK == DIAG_BLOCKS_PER_TICK - 1:
                    queue.tick()
            a_diag = _dot(jnp.concatenate(blocks, axis=0).astype(BF16), spread_ref[...])
            att = jnp.where(level == nlev, a_diag, att)
            queue.tick()

            v = v_scr[rows, vs]
            s = s_scr[hd]
            o = _dot(q_in, s.astype(BF16)) + _dot(att.astype(BF16), v)
            on = _rms(o, ghead_ref[:, vs])
            og_scr[rows, vs] = (on * r_scr[rows, vs].astype(F32)).astype(BF16)
            d_col = jnp.sum(jnp.where(eye, d_last, 0.0), axis=1, keepdims=True)
            kv = lax.dot_general(k_st, v, tn, preferred_element_type=F32)
            s_scr[hd] = d_col * s + kv

    def run_phase(chunks, items):
        ticks_per_head = 2 + (chunk // SUBLANES) // DIAG_BLOCKS_PER_TICK
        queue = _WorkQueue(items, len(chunks) * (1 + N_HEADS * ticks_per_head))
        for c in chunks:
            run_chunk(c, queue)
        queue.flush()

    def blocks(lo, hi, step):
        return [slice(r, r + step) for r in range(lo, hi, step)]

    if pipelined:
        half = tile // 2
        first, second = blocks(0, half, ROW_BLOCK), blocks(half, tile, ROW_BLOCK)

        @pl.when(i == 0)
        def _():
            s_scr[...] = s0_ref[0]
            for rows in first:
                for item in proj_items(x_ref, rows, rows):
                    item()

        run_phase(range(0, n_chunks // 2),
                  [it for rows in second for it in proj_items(x_ref, rows, rows)])
        run_phase(range(n_chunks // 2, n_chunks),
                  [it for rows in first for it in proj_items(xn_ref, rows, rows)]
                  + [it for rows in first for it in out_items(rows)])
        for rows in second:
            for item in out_items(rows):
                item()
    else:
        @pl.when(i == 0)
        def _():
            s_scr[...] = s0_ref[0]

        whole = slice(0, tile)
        for item in proj_items(x_ref, whole, whole):
            item()
        run_phase(range(n_chunks), [])
        for item in out_items(whole):
            item()

    @pl.when(i == pl.num_programs(1) - 1)
    def _():
        sout_ref[0] = s_scr[...]


def _conv_kernel(x_ref, hist_ref, gpre_ref, gpost_ref, win_ref, cw_ref, wout_ref,
                 y_ref, hout_ref,
                 xc_scr,
                 *, tile):
    i = pl.program_id(1)

    @pl.when(i == 0)
    def _():
        xc_scr[0:HIST_ROW, :] = hist_ref[0]

    x = x_ref[0]
    h = _rms(x, gpre_ref[...]).astype(BF16)
    n_out = D_MODEL // MXU_COLS
    acc = [jnp.zeros((tile, MXU_COLS), F32) for _ in range(n_out)]
    n_cb = D_INNER // CONV_COLS
    for cb in range(n_cb):
        cols = slice(cb * CONV_COLS, (cb + 1) * CONV_COLS)

        def proj(part):
            return _dot(h, win_ref[part * n_cb + cb])

        xc = proj(1) * proj(2)
        xc_scr[HIST_ROW:HIST_ROW + tile, cols] = xc
        x2 = xc_scr[HIST_ROW - 2:HIST_ROW - 2 + tile, cols]
        x1 = xc_scr[HIST_ROW - 1:HIST_ROW - 1 + tile, cols]
        conv = x2 * cw_ref[0:1, cols] + x1 * cw_ref[1:2, cols] + xc * cw_ref[2:3, cols]
        yb = (proj(0) * conv * _silu(proj(3))).astype(BF16)
        for n in range(n_out):
            acc[n] = acc[n] + _dot(yb, wout_ref[n, cols, :])
        xc_scr[0:HIST_ROW, cols] = xc[tile - HIST_ROW:tile]
    y_ref[0] = x + _rms(jnp.concatenate(acc, axis=1), gpost_ref[...])

    @pl.when(i == pl.num_programs(1) - 1)
    def _():
        hout_ref[0] = xc_scr[0:HIST_ROW, :]


def _tile_sizes(t_len):
    tile = min(t_len, MAX_TILE)
    chunk = min(tile, MAX_CHUNK)
    assert t_len % tile == 0 and tile % chunk == 0 and chunk % SUBLANES == 0
    return tile, chunk


def _const_spec(shape):
    return pl.BlockSpec(shape, lambda b, i: (0,) * len(shape), pipeline_mode=pl.Buffered(1))


def _layer_spec(stacked, li):
    rest = stacked.shape[1:]
    return pl.BlockSpec((None,) + rest, lambda b, i: (li,) + (0,) * len(rest),
                        pipeline_mode=pl.Buffered(1))


def _cast_kernel(w_ref, o_ref):
    o_ref[0, 0] = w_ref[0].astype(BF16)


def _col_pieces(w, width):
    n_layers, k_dim, n_dim = w.shape
    return pl.pallas_call(
        _cast_kernel,
        grid=(n_layers, n_dim // width),
        in_specs=[pl.BlockSpec((1, k_dim, width), lambda l, n: (l, 0, n))],
        out_specs=pl.BlockSpec((1, 1, k_dim, width), lambda l, n: (l, n, 0, 0)),
        out_shape=jax.ShapeDtypeStruct((n_layers, n_dim // width, k_dim, width), BF16),
        compiler_params=pltpu.CompilerParams(dimension_semantics=("arbitrary", "arbitrary")),
        name="weight_pieces",
    )(w)


def _gla_layer(x, s0, li, gpre, gpost, win, wf1, wf2, bf, ghead, wout):
    bsz, t_len, _ = x.shape
    tile, chunk = _tile_sizes(t_len)
    nlev, mall, level, spread = _chunk_constants(chunk)
    n_tiles = t_len // tile
    n_chunks = tile // chunk
    half = tile // 2
    pipelined = n_chunks % 2 == 0 and half % ROW_BLOCK == 0
    kern = functools.partial(_gla_kernel, tile=tile, chunk=chunk, nlev=nlev, pipelined=pipelined)
    tok_spec = pl.BlockSpec((1, tile, D_MODEL), lambda b, i: (b, i, 0))
    next_spec = pl.BlockSpec((1, half, D_MODEL),
                             lambda b, i: (b, jnp.minimum(2 * i + 2, 2 * n_tiles - 1), 0))
    st_spec = pl.BlockSpec((1, N_HEADS, HEAD_K, HEAD_V), lambda b, i: (b, 0, 0, 0))
    return pl.pallas_call(
        kern,
        grid=(bsz, n_tiles),
        in_specs=[
            tok_spec, next_spec, st_spec,
            _const_spec((1, D_MODEL)), _const_spec((1, D_MODEL)),
            _layer_spec(win, li),
            _const_spec((D_MODEL, LANES)), _const_spec((LANES, DK_TOT)),
            _const_spec((1, DK_TOT)), _const_spec((1, DV_TOT)),
            _layer_spec(wout, li),
            _const_spec(mall.shape), _const_spec((chunk, chunk)),
            _const_spec((LANES, chunk)),
        ],
        out_specs=[tok_spec, st_spec],
        out_shape=[jax.ShapeDtypeStruct(x.shape, F32),
                   jax.ShapeDtypeStruct(s0.shape, F32)],
        scratch_shapes=[
            pltpu.VMEM((N_HEADS, HEAD_K, HEAD_V), F32),
            pltpu.VMEM((tile, D_MODEL), BF16),
            pltpu.VMEM((N_HEADS, tile, HEAD_K), F32),
            pltpu.VMEM((N_HEADS, tile, HEAD_K), F32),
            pltpu.VMEM((tile, DK_TOT), F32),
            pltpu.VMEM((tile, DV_TOT), BF16),
            pltpu.VMEM((tile, DV_TOT), BF16),
            pltpu.VMEM((tile, DV_TOT), BF16),
            pltpu.VMEM((N_HEADS, chunk, HEAD_K), F32),
            pltpu.VMEM((tile, D_MODEL), F32),
        ],
        compiler_params=pltpu.CompilerParams(
            dimension_semantics=("arbitrary", "arbitrary"),
            vmem_limit_bytes=VMEM_LIMIT_BYTES),
        name="gla_layer",
    )(x, x, s0, gpre, gpost, win, wf1, wf2, bf, ghead, wout, mall, level, spread)


def _conv_layer(x, hist, li, gpre, gpost, win, cw, wout):
    bsz, t_len, _ = x.shape
    tile, _ = _tile_sizes(t_len)
    assert tile >= HIST_ROW
    hist8 = jnp.pad(hist, ((0, 0), (HIST_ROW - (CONV_W - 1), 0), (0, 0)))
    kern = functools.partial(_conv_kernel, tile=tile)
    tok_spec = pl.BlockSpec((1, tile, D_MODEL), lambda b, i: (b, i, 0))
    hist_spec = pl.BlockSpec((1, HIST_ROW, D_INNER), lambda b, i: (b, 0, 0))
    y, hout = pl.pallas_call(
        kern,
        grid=(bsz, t_len // tile),
        in_specs=[
            tok_spec, hist_spec,
            _const_spec((1, D_MODEL)), _const_spec((1, D_MODEL)),
            _layer_spec(win, li),
            _const_spec((CONV_W, D_INNER)),
            _layer_spec(wout, li),
        ],
        out_specs=[tok_spec, hist_spec],
        out_shape=[jax.ShapeDtypeStruct(x.shape, F32),
                   jax.ShapeDtypeStruct((bsz, HIST_ROW, D_INNER), F32)],
        scratch_shapes=[pltpu.VMEM((HIST_ROW + tile, D_INNER), F32)],
        compiler_params=pltpu.CompilerParams(
            dimension_semantics=("arbitrary", "arbitrary"),
            vmem_limit_bytes=VMEM_LIMIT_BYTES),
        name="conv_layer",
    )(x, hist8, gpre, gpost, win, cw, wout)
    return y, hout[:, HIST_ROW - (CONV_W - 1):]


def kernel(x_prompt, x_sample, state_gla, state_conv, norm_pre, norm_post, gla_w_in, gla_w_f1, gla_w_f2, gla_b_f, gla_g_head, gla_w_out, conv_w_in, conv_w, conv_w_out):
    depth = norm_pre.shape[0]
    xp, xs = x_prompt, x_sample
    s0_prompt = jnp.zeros((xp.shape[0], N_HEADS, HEAD_K, HEAD_V), F32)
    hist_prompt = jnp.zeros((xp.shape[0], CONV_W - 1, D_INNER), F32)
    gla_p, gla_s, conv_p, conv_s = [], [], [], []
    gla_win = _col_pieces(gla_w_in, MXU_COLS)
    gla_wout = _col_pieces(gla_w_out, MXU_COLS)
    conv_win = _col_pieces(conv_w_in, CONV_COLS)
    conv_wout = _col_pieces(conv_w_out, MXU_COLS)
    for layer in range(depth):
        li = layer // 2
        gpre = norm_pre[layer][None]
        gpost = norm_post[layer][None]
        if layer % 2 == 0:
            rank = gla_w_f1.shape[-1]
            prm = (gpre, gpost,
                   gla_win,
                   jnp.pad(gla_w_f1[li], ((0, 0), (0, LANES - rank))).astype(BF16),
                   jnp.pad(gla_w_f2[li], ((0, LANES - rank), (0, 0))).astype(BF16),
                   gla_b_f[li][None], gla_g_head[li][None],
                   gla_wout)
            xp, sp = _gla_layer(xp, s0_prompt, li, *prm)
            xs, ss = _gla_layer(xs, state_gla[li], li, *prm)
            gla_p.append(sp)
            gla_s.append(ss)
        else:
            prm = (gpre, gpost, conv_win, conv_w[li], conv_wout)
            xp, sp = _conv_layer(xp, hist_prompt, li, *prm)
            xs, ss = _conv_layer(xs, state_conv[li], li, *prm)
            conv_p.append(sp)
            conv_s.append(ss)
    return (xp, xs, jnp.stack(gla_p), jnp.stack(conv_p), jnp.stack(gla_s), jnp.stack(conv_s))
```

```python
import functools

import numpy as np
import jax
import jax.numpy as jnp
from jax import lax
from jax.experimental import pallas as pl
from jax.experimental.pallas import tpu as pltpu

D_MODEL = 1024
N_HEADS = 4
HEAD_K = 128
HEAD_V = 512
DK_TOT = N_HEADS * HEAD_K
DV_TOT = N_HEADS * HEAD_V
D_INNER = 2 * D_MODEL
GATE_NORM = 16.0
CONV_W = 3
EPS = 1e-6
LOG2_E = 1.4426950408889634

SUBLANES = 8
LANES = 128
MXU_COLS = 256
VMEM_LIMIT_BYTES = 56 * 1024 * 1024

MAX_TILE = 512
MAX_CHUNK = 128
ROW_BLOCK = 128
DIAG_BLOCKS_PER_TICK = 2
CONV_COLS = 512
HIST_ROW = SUBLANES

F32 = jnp.float32
BF16 = jnp.bfloat16


def _dot(a, b):
    return jnp.dot(a, b, preferred_element_type=F32)


def _rms(x, g):
    return x * lax.rsqrt(jnp.mean(x * x, axis=-1, keepdims=True) + EPS) * g


def _silu(x):
    return x / (1.0 + jnp.exp(-x))


def _row_bcast(ref, lead, r):
    return ref[lead, pl.ds(r, SUBLANES, stride=0), :]


def _chunk_constants(chunk):
    nlev = int(np.log2(chunk // SUBLANES))
    assert SUBLANES << nlev == chunk
    r = np.arange(chunk)[:, None]
    u = np.arange(chunk)[None, :]
    tri = (u <= r).astype(np.float32)
    tri = np.concatenate([tri, tri], axis=1)
    level = np.full((chunk, chunk), -1, np.int32)
    for li in range(nlev):
        half = chunk >> (li + 1)
        blk = 2 * half
        level[(r // blk == u // blk) & ((r % blk) >= half) & ((u % blk) < half)] = li
    level[(r // SUBLANES == u // SUBLANES) & (u <= r)] = nlev
    lane = np.arange(LANES)[:, None]
    spread = ((lane < SUBLANES) & (u % SUBLANES == lane)).astype(np.float32)
    return nlev, jnp.asarray(tri, BF16), jnp.asarray(level), jnp.asarray(spread, BF16)


class _WorkQueue:
    def __init__(self, items, n_ticks):
        self.items = list(items)
        self.total = len(self.items)
        self.n_ticks = max(n_ticks, 1)
        self.done = 0
        self.ticks = 0

    def tick(self):
        self.ticks += 1
        target = min(self.total, -(-self.total * self.ticks // self.n_ticks))
        while self.done < target:
            self.items[self.done]()
            self.done += 1

    def flush(self):
        while self.done < self.total:
            self.items[self.done]()
            self.done += 1


def _gla_kernel(x_ref, xn_ref, s0_ref, gpre_ref, gpost_ref, win_ref, wf1_ref, wf2_ref, bf_ref,
                ghead_ref, wout_ref, tri_ref, level_ref, spread_ref,
                y_ref, sout_ref,
                s_scr, h_scr, q_scr, k_scr, g_scr, v_scr, r_scr, og_scr, bc_scr, y_scr,
                *, tile, chunk, nlev, pipelined):
    i = pl.program_id(1)
    n_chunks = tile // chunk
    lane = lax.broadcasted_iota(jnp.int32, (SUBLANES, LANES), 1)
    eye = (lax.broadcasted_iota(jnp.int32, (HEAD_K, HEAD_K), 0)
           == lax.broadcasted_iota(jnp.int32, (HEAD_K, HEAD_K), 1))
    nt = (((1,), (1,)), ((), ()))
    tn = (((0,), (0,)), ((), ()))

    def proj_items(src_ref, src_rows, rows):
        def norm():
            h_scr[rows, :] = _rms(src_ref[0, src_rows, :], gpre_ref[...]).astype(BF16)

        def cols(lo):
            return _dot(h_scr[rows, :], win_ref[lo // MXU_COLS])

        def q_item(pair):
            def f():
                res = cols(pair * MXU_COLS) * (HEAD_K ** -0.5)
                q_scr[2 * pair, rows, :] = res[:, :HEAD_K]
                q_scr[2 * pair + 1, rows, :] = res[:, HEAD_K:]
            return f

        def k_item(pair):
            def f():
                res = cols(DK_TOT + pair * MXU_COLS)
                k_scr[2 * pair, rows, :] = res[:, :HEAD_K]
                k_scr[2 * pair + 1, rows, :] = res[:, HEAD_K:]
            return f

        def v_item(n):
            def f():
                v_scr[rows, n * MXU_COLS:(n + 1) * MXU_COLS] = cols(
                    2 * DK_TOT + n * MXU_COLS).astype(BF16)
            return f

        def r_item(n):
            def f():
                r_scr[rows, n * MXU_COLS:(n + 1) * MXU_COLS] = _silu(cols(
                    2 * DK_TOT + DV_TOT + n * MXU_COLS)).astype(BF16)
            return f

        def gate():
            hf = _dot(h_scr[rows, :], wf1_ref[...]).astype(BF16)
            xg = _dot(hf, wf2_ref[...]) + bf_ref[...]
            g_scr[rows, :] = ((jnp.minimum(xg, 0.0) - jnp.log1p(jnp.exp(-jnp.abs(xg))))
                              * (LOG2_E / GATE_NORM))

        items = [norm, gate]
        items += [q_item(p) for p in range(DK_TOT // MXU_COLS)]
        items += [k_item(p) for p in range(DK_TOT // MXU_COLS)]
        for n in range(DV_TOT // MXU_COLS):
            items += [v_item(n), r_item(n)]
        return items

    def out_items(rows):
        def o_item(n):
            def f():
                y_scr[rows, n * MXU_COLS:(n + 1) * MXU_COLS] = _dot(og_scr[rows, :], wout_ref[n])
            return f

        def post():
            y_ref[0, rows, :] = x_ref[0, rows, :] + _rms(y_scr[rows, :], gpost_ref[...])

        return [o_item(n) for n in range(D_MODEL // MXU_COLS)] + [post]

    def run_chunk(c, queue):
        if isinstance(c, int):
            base = c * chunk
            rows = slice(base, base + chunk)
        else:
            base = pl.multiple_of(c * chunk, chunk)
            rows = pl.ds(base, chunk)
        g = g_scr[rows, :]
        g_hi = g.astype(BF16)
        g_lo = (g - g_hi.astype(F32)).astype(BF16)
        bcum = _dot(tri_ref[...], jnp.concatenate([g_hi, g_lo], axis=0))
        level = level_ref[...]
        queue.tick()
        n_blocks = chunk // SUBLANES
        for hd in range(N_HEADS):
            ks = slice(hd * HEAD_K, (hd + 1) * HEAD_K)
            vs = slice(hd * HEAD_V, (hd + 1) * HEAD_V)
            q = q_scr[hd, rows, :]
            k = k_scr[hd, rows, :]
            b = bcum[:, ks]
            bc_scr[hd] = b
            b_blk = [b[r0:r0 + SUBLANES] for r0 in range(0, chunk, SUBLANES)]
            b_last = _row_bcast(bc_scr, hd, chunk - 1)
            q_in = (q * jnp.exp2(b)).astype(BF16)
            k_st = (k * jnp.exp2(jnp.concatenate([b_last - bb for bb in b_blk], axis=0))).astype(BF16)
            d_last = jnp.exp2(b_last[0:1])

            att = jnp.zeros((chunk, chunk), F32)
            for li in range(nlev):
                half = chunk >> (li + 1)
                src, expo = [], []
                for bi in range(n_blocks):
                    r0 = bi * SUBLANES
                    start = r0 - r0 % (2 * half)
                    b_mid = _row_bcast(bc_scr, hd, start + half - 1)
                    if r0 - start >= half:
                        src.append(q[r0:r0 + SUBLANES])
                        expo.append(b_blk[bi] - b_mid)
                    else:
                        src.append(k[r0:r0 + SUBLANES])
                        expo.append(b_mid - b_blk[bi])
                w = (jnp.concatenate(src, axis=0)
                     * jnp.exp2(jnp.concatenate(expo, axis=0))).astype(BF16)
                att = jnp.where(level == li,
                                lax.dot_general(w, w, nt, preferred_element_type=F32), att)
            queue.tick()

            blocks = []
            for bi in range(n_blocks):
                r0 = bi * SUBLANES
                qb = q[r0:r0 + SUBLANES]
                bb = b_blk[bi]
                blk = jnp.zeros((SUBLANES, LANES), F32)
                for j in range(SUBLANES):
                    k_row = _row_bcast(k_scr, hd, base + r0 + j)
                    b_row = _row_bcast(bc_scr, hd, r0 + j)
                    p = qb * (k_row * jnp.exp2(bb - b_row))
                    blk = jnp.where(lane == j, jnp.sum(p, axis=1, keepdims=True), blk)
                blocks.append(blk)
                if bi % DIAG_BLOCKS_PER_TICK == DIAG_BLOCKS_PER_TICK - 1:
                    queue.tick()
            a_diag = _dot(jnp.concatenate(blocks, axis=0).astype(BF16), spread_ref[...])
            att = jnp.where(level == nlev, a_diag, att)
            queue.tick()

            v = v_scr[rows, vs]
            s = s_scr[hd]
            o = _dot(q_in, s.astype(BF16)) + _dot(att.astype(BF16), v)
            on = _rms(o, ghead_ref[:, vs])
            og_scr[rows, vs] = (on * r_scr[rows, vs].astype(F32)).astype(BF16)
            d_col = jnp.sum(jnp.where(eye, d_last, 0.0), axis=1, keepdims=True)
            kv = lax.dot_general(k_st, v, tn, preferred_element_type=F32)
            s_scr[hd] = d_col * s + kv

    def run_phase(chunks, items):
        ticks_per_head = 2 + (chunk // SUBLANES) // DIAG_BLOCKS_PER_TICK
        queue = _WorkQueue(items, len(chunks) * (1 + N_HEADS * ticks_per_head))
        for c in chunks:
            run_chunk(c, queue)
        queue.flush()

    def blocks(lo, hi, step):
        return [slice(r, r + step) for r in range(lo, hi, step)]

    if pipelined:
        half = tile // 2
        first, second = blocks(0, half, ROW_BLOCK), blocks(half, tile, ROW_BLOCK)

        @pl.when(i == 0)
        def _():
            s_scr[...] = s0_ref[0]
            for rows in first:
                for item in proj_items(x_ref, rows, rows):
                    item()

        run_phase(range(0, n_chunks // 2),
                  [it for rows in second for it in proj_items(x_ref, rows, rows)])
        run_phase(range(n_chunks // 2, n_chunks),
                  [it for rows in first for it in proj_items(xn_ref, rows, rows)]
                  + [it for rows in first for it in out_items(rows)])
        for rows in second:
            for item in out_items(rows):
                item()
        @pl.when(i == pl.num_programs(1) - 1)
        def _():
            sout_ref[0] = s_scr[...]
    else:
        whole = slice(0, tile)

        @pl.when(i == 0)
        def _():
            for item in proj_items(x_ref, whole, whole):
                item()

        s_scr[...] = s0_ref[0]
        run_phase([i], [])
        sout_ref[0] = s_scr[...]

        @pl.when(i == pl.num_programs(1) - 1)
        def _():
            for item in out_items(whole):
                item()


def _conv_kernel(x_ref, hist_ref, gpre_ref, gpost_ref, win_ref, cw_ref, wout_ref,
                 y_ref, hout_ref,
                 xc_scr,
                 *, tile, seg):
    i = pl.program_id(1)

    if seg is None:
        @pl.when(i == 0)
        def _():
            xc_scr[0:HIST_ROW, :] = hist_ref[0]
    else:
        xc_scr[0:HIST_ROW, :] = jnp.zeros((HIST_ROW, D_INNER), F32)
        pos = lax.broadcasted_iota(jnp.int32, (tile, CONV_COLS), 0) & (seg - 1)

    x = x_ref[0]
    h = _rms(x, gpre_ref[...]).astype(BF16)
    n_out = D_MODEL // MXU_COLS
    acc = [jnp.zeros((tile, MXU_COLS), F32) for _ in range(n_out)]
    n_cb = D_INNER // CONV_COLS
    for cb in range(n_cb):
        cols = slice(cb * CONV_COLS, (cb + 1) * CONV_COLS)

        def proj(part):
            return _dot(h, win_ref[part * n_cb + cb])

        xc = proj(1) * proj(2)
        xc_scr[HIST_ROW:HIST_ROW + tile, cols] = xc
        x2 = xc_scr[HIST_ROW - 2:HIST_ROW - 2 + tile, cols]
        x1 = xc_scr[HIST_ROW - 1:HIST_ROW - 1 + tile, cols]
        if seg is not None:
            x1 = jnp.where(pos < 1, hist_ref[0, 0, :, cols], x1)
            x2 = jnp.where(pos < 2, hist_ref[0, 1, :, cols], x2)
            hout_ref[0, :, cols] = xc
        conv = x2 * cw_ref[0:1, cols] + x1 * cw_ref[1:2, cols] + xc * cw_ref[2:3, cols]
        yb = (proj(0) * conv * _silu(proj(3))).astype(BF16)
        for n in range(n_out):
            acc[n] = acc[n] + _dot(yb, wout_ref[n, cols, :])
        if seg is None:
            xc_scr[0:HIST_ROW, cols] = xc[tile - HIST_ROW:tile]
    y_ref[0] = x + _rms(jnp.concatenate(acc, axis=1), gpost_ref[...])

    if seg is None:
        @pl.when(i == pl.num_programs(1) - 1)
        def _():
            hout_ref[0] = xc_scr[0:HIST_ROW, :]


def _tile_sizes(t_len):
    tile = min(t_len, MAX_TILE)
    chunk = min(tile, MAX_CHUNK)
    assert t_len % tile == 0 and tile % chunk == 0 and chunk % SUBLANES == 0
    return tile, chunk


def _const_spec(shape):
    return pl.BlockSpec(shape, lambda b, i: (0,) * len(shape), pipeline_mode=pl.Buffered(1))


def _layer_spec(stacked, li):
    rest = stacked.shape[1:]
    return pl.BlockSpec((None,) + rest, lambda b, i: (li,) + (0,) * len(rest),
                        pipeline_mode=pl.Buffered(1))


def _cast_kernel(w_ref, o_ref):
    o_ref[0, 0] = w_ref[0].astype(BF16)


def _col_pieces(w, width):
    n_layers, k_dim, n_dim = w.shape
    return pl.pallas_call(
        _cast_kernel,
        grid=(n_layers, n_dim // width),
        in_specs=[pl.BlockSpec((1, k_dim, width), lambda l, n: (l, 0, n))],
        out_specs=pl.BlockSpec((1, 1, k_dim, width), lambda l, n: (l, n, 0, 0)),
        out_shape=jax.ShapeDtypeStruct((n_layers, n_dim // width, k_dim, width), BF16),
        compiler_params=pltpu.CompilerParams(dimension_semantics=("arbitrary", "arbitrary")),
        name="weight_pieces",
    )(w)


def _gla_layer(x, s0, s0_li, li, gpre, gpost, win, wf1, wf2, bf, ghead, wout):
    bsz, t_len, _ = x.shape
    batched = t_len <= MAX_CHUNK and bsz * t_len <= MAX_TILE
    if batched:
        tile, chunk = bsz * t_len, t_len
        x_in = x.reshape(1, tile, D_MODEL)
        grid = (1, bsz)
        tok_spec = pl.BlockSpec((1, tile, D_MODEL), lambda b, i: (0, 0, 0))
        st_spec = pl.BlockSpec((1, N_HEADS, HEAD_K, HEAD_V), lambda b, i: (i, 0, 0, 0))
        st_in_spec = pl.BlockSpec((None, 1, N_HEADS, HEAD_K, HEAD_V),
                                  lambda b, i: (s0_li, i, 0, 0, 0))
        n_tiles = 1
    else:
        tile, chunk = _tile_sizes(t_len)
        n_tiles = t_len // tile
        assert (tile // chunk) % 2 == 0 and (tile // 2) % ROW_BLOCK == 0
        x_in = x
        grid = (bsz, n_tiles)
        tok_spec = pl.BlockSpec((1, tile, D_MODEL), lambda b, i: (b, i, 0))
        st_spec = pl.BlockSpec((1, N_HEADS, HEAD_K, HEAD_V), lambda b, i: (b, 0, 0, 0))
        st_in_spec = pl.BlockSpec((None, 1, N_HEADS, HEAD_K, HEAD_V),
                                  lambda b, i: (s0_li, b, 0, 0, 0))
    nlev, tri, level, spread = _chunk_constants(chunk)
    half = tile // 2
    kern = functools.partial(_gla_kernel, tile=tile, chunk=chunk, nlev=nlev,
                             pipelined=not batched)
    next_spec = pl.BlockSpec(
        (1, half, D_MODEL),
        lambda b, i: (0 if batched else b, jnp.minimum(2 * i + 2, 2 * n_tiles - 1), 0))
    y, s_out = pl.pallas_call(
        kern,
        grid=grid,
        in_specs=[
            tok_spec, next_spec, st_in_spec,
            _const_spec((1, D_MODEL)), _const_spec((1, D_MODEL)),
            _layer_spec(win, li),
            _const_spec((D_MODEL, LANES)), _const_spec((LANES, DK_TOT)),
            _const_spec((1, DK_TOT)), _const_spec((1, DV_TOT)),
            _layer_spec(wout, li),
            _const_spec(tri.shape), _const_spec((chunk, chunk)),
            _const_spec((LANES, chunk)),
        ],
        out_specs=[tok_spec, st_spec],
        out_shape=[jax.ShapeDtypeStruct(x_in.shape, F32),
                   jax.ShapeDtypeStruct(s0.shape[1:], F32)],
        scratch_shapes=[
            pltpu.VMEM((N_HEADS, HEAD_K, HEAD_V), F32),
            pltpu.VMEM((tile, D_MODEL), BF16),
            pltpu.VMEM((N_HEADS, tile, HEAD_K), F32),
            pltpu.VMEM((N_HEADS, tile, HEAD_K), F32),
            pltpu.VMEM((tile, DK_TOT), F32),
            pltpu.VMEM((tile, DV_TOT), BF16),
            pltpu.VMEM((tile, DV_TOT), BF16),
            pltpu.VMEM((tile, DV_TOT), BF16),
            pltpu.VMEM((N_HEADS, chunk, HEAD_K), F32),
            pltpu.VMEM((tile, D_MODEL), F32),
        ],
        compiler_params=pltpu.CompilerParams(
            dimension_semantics=("arbitrary", "arbitrary"),
            vmem_limit_bytes=VMEM_LIMIT_BYTES),
        name="gla_layer",
    )(x_in, x_in, s0, gpre, gpost, win, wf1, wf2, bf, ghead, wout, tri, level, spread)
    return y.reshape(x.shape), s_out


def _conv_layer(x, hist, li, gpre, gpost, win, cw, wout):
    bsz, t_len, _ = x.shape
    n_hist = CONV_W - 1
    segmented = bsz * t_len <= MAX_TILE and t_len & (t_len - 1) == 0 and t_len >= n_hist
    if segmented:
        tile = bsz * t_len
        taps = [jnp.pad(hist[:, n_hist - d:], ((0, 0), (0, t_len - d), (0, 0))).reshape(tile, D_INNER)
                for d in range(1, CONV_W)]
        hist_in = jnp.stack(taps)[None]
        x_in = x.reshape(1, tile, D_MODEL)
        grid = (1, 1)
        hist_spec = pl.BlockSpec((1, n_hist, tile, D_INNER), lambda b, i: (0, 0, 0, 0))
        hout_spec = pl.BlockSpec((1, tile, D_INNER), lambda b, i: (0, 0, 0))
        hout_shape = (1, tile, D_INNER)
        seg = t_len
    else:
        tile, _ = _tile_sizes(t_len)
        assert tile >= HIST_ROW
        hist_in = jnp.pad(hist, ((0, 0), (HIST_ROW - n_hist, 0), (0, 0)))
        x_in = x
        grid = (bsz, t_len // tile)
        hist_spec = hout_spec = pl.BlockSpec((1, HIST_ROW, D_INNER), lambda b, i: (b, 0, 0))
        hout_shape = (bsz, HIST_ROW, D_INNER)
        seg = None
    kern = functools.partial(_conv_kernel, tile=tile, seg=seg)
    tok_spec = pl.BlockSpec((1, tile, D_MODEL), lambda b, i: (b, i, 0))
    y, hout = pl.pallas_call(
        kern,
        grid=grid,
        in_specs=[
            tok_spec, hist_spec,
            _const_spec((1, D_MODEL)), _const_spec((1, D_MODEL)),
            _layer_spec(win, li),
            _const_spec((CONV_W, D_INNER)),
            _layer_spec(wout, li),
        ],
        out_specs=[tok_spec, hout_spec],
        out_shape=[jax.ShapeDtypeStruct(x_in.shape, F32),
                   jax.ShapeDtypeStruct(hout_shape, F32)],
        scratch_shapes=[pltpu.VMEM((HIST_ROW + tile, D_INNER), F32)],
        compiler_params=pltpu.CompilerParams(
            dimension_semantics=("arbitrary", "arbitrary"),
            vmem_limit_bytes=VMEM_LIMIT_BYTES),
        name="conv_layer",
    )(x_in, hist_in, gpre, gpost, win, cw, wout)
    if segmented:
        return (y.reshape(x.shape),
                hout.reshape(bsz, t_len, D_INNER)[:, t_len - n_hist:])
    return y, hout[:, HIST_ROW - n_hist:]


def kernel(x_prompt, x_sample, state_gla, state_conv, norm_pre, norm_post, gla_w_in, gla_w_f1, gla_w_f2, gla_b_f, gla_g_head, gla_w_out, conv_w_in, conv_w, conv_w_out):
    depth = norm_pre.shape[0]
    xp, xs = x_prompt, x_sample
    s0_prompt = jnp.zeros((1, xp.shape[0], N_HEADS, HEAD_K, HEAD_V), F32)
    hist_prompt = jnp.zeros((xp.shape[0], CONV_W - 1, D_INNER), F32)
    gla_p, gla_s, conv_p, conv_s = [], [], [], []
    gla_win = _col_pieces(gla_w_in, MXU_COLS)
    gla_wout = _col_pieces(gla_w_out, MXU_COLS)
    conv_win = _col_pieces(conv_w_in, CONV_COLS)
    conv_wout = _col_pieces(conv_w_out, MXU_COLS)
    for layer in range(depth):
        li = layer // 2
        gpre = norm_pre[layer][None]
        gpost = norm_post[layer][None]
        if layer % 2 == 0:
            rank = gla_w_f1.shape[-1]
            prm = (gpre, gpost,
                   gla_win,
                   jnp.pad(gla_w_f1[li], ((0, 0), (0, LANES - rank))).astype(BF16),
                   jnp.pad(gla_w_f2[li], ((0, LANES - rank), (0, 0))).astype(BF16),
                   gla_b_f[li][None], gla_g_head[li][None],
                   gla_wout)
            xp, sp = _gla_layer(xp, s0_prompt, 0, li, *prm)
            xs, ss = _gla_layer(xs, state_gla, li, li, *prm)
            gla_p.append(sp)
            gla_s.append(ss)
        else:
            prm = (gpre, gpost, conv_win, conv_w[li], conv_wout)
            xp, sp = _conv_layer(xp, hist_prompt, li, *prm)
            xs, ss = _conv_layer(xs, state_conv[li], li, *prm)
            conv_p.append(sp)
            conv_s.append(ss)
    return (xp, xs, jnp.stack(gla_p), jnp.stack(conv_p), jnp.stack(gla_s), jnp.stack(conv_s))
```

```python
import functools

import numpy as np
import jax
import jax.numpy as jnp
from jax import lax
from jax.experimental import pallas as pl
from jax.experimental.pallas import tpu as pltpu

D_MODEL = 1024
N_HEADS = 4
HEAD_K = 128
HEAD_V = 512
DK_TOT = N_HEADS * HEAD_K
DV_TOT = N_HEADS * HEAD_V
D_INNER = 2 * D_MODEL
GATE_NORM = 16.0
CONV_W = 3
EPS = 1e-6
LOG2_E = 1.4426950408889634

SUBLANES = 8
LANES = 128
MXU_COLS = 256
VMEM_LIMIT_BYTES = 56 * 1024 * 1024

MAX_TILE = 512
MAX_CHUNK = 128
ROW_BLOCK = 256
DIAG_BLOCKS_PER_TICK = 2
CONV_COLS = 512
HIST_ROW = SUBLANES

F32 = jnp.float32
BF16 = jnp.bfloat16


def _dot(a, b):
    return jnp.dot(a, b, preferred_element_type=F32)


def _rms(x, g):
    return x * lax.rsqrt(jnp.mean(x * x, axis=-1, keepdims=True) + EPS) * g


def _silu(x):
    return x / (1.0 + jnp.exp(-x))


def _row_bcast(ref, lead, r):
    return ref[lead, pl.ds(r, SUBLANES, stride=0), :]


def _chunk_constants(chunk):
    nlev = int(np.log2(chunk // SUBLANES))
    assert SUBLANES << nlev == chunk
    r = np.arange(chunk)[:, None]
    u = np.arange(chunk)[None, :]
    tri = (u <= r).astype(np.float32)
    tri = np.concatenate([tri, tri], axis=1)
    level = np.full((chunk, chunk), -1, np.int32)
    for li in range(nlev):
        half = chunk >> (li + 1)
        blk = 2 * half
        level[(r // blk == u // blk) & ((r % blk) >= half) & ((u % blk) < half)] = li
    level[(r // SUBLANES == u // SUBLANES) & (u <= r)] = nlev
    lane = np.arange(LANES)[:, None]
    spread = ((lane < SUBLANES) & (u % SUBLANES == lane)).astype(np.float32)
    return nlev, jnp.asarray(tri, BF16), jnp.asarray(level), jnp.asarray(spread, BF16)


class _WorkQueue:
    def __init__(self, items, n_ticks):
        self.items = list(items)
        self.total = len(self.items)
        self.n_ticks = max(n_ticks, 1)
        self.done = 0
        self.ticks = 0

    def tick(self):
        self.ticks += 1
        target = min(self.total, -(-self.total * self.ticks // self.n_ticks))
        while self.done < target:
            self.items[self.done]()
            self.done += 1

    def flush(self):
        while self.done < self.total:
            self.items[self.done]()
            self.done += 1


def _gla_kernel(x_ref, xn_ref, s0_ref, gpre_ref, gpost_ref, win_ref, wf1_ref, wf2_ref, bf_ref,
                ghead_ref, wout_ref, tri_ref, level_ref, spread_ref,
                y_ref, sout_ref,
                s_scr, h_scr, q_scr, k_scr, g_scr, v_scr, r_scr, og_scr, bc_scr, y_scr,
                *, tile, chunk, nlev, pipelined):
    i = pl.program_id(1)
    n_chunks = tile // chunk
    lane = lax.broadcasted_iota(jnp.int32, (SUBLANES, LANES), 1)
    eye = (lax.broadcasted_iota(jnp.int32, (HEAD_K, HEAD_K), 0)
           == lax.broadcasted_iota(jnp.int32, (HEAD_K, HEAD_K), 1))
    nt = (((1,), (1,)), ((), ()))
    tn = (((0,), (0,)), ((), ()))

    def proj_items(src_ref, src_rows, rows):
        def norm():
            h_scr[rows, :] = _rms(src_ref[0, src_rows, :], gpre_ref[...]).astype(BF16)

        def cols(lo):
            return _dot(h_scr[rows, :], win_ref[lo // MXU_COLS])

        def q_item(pair):
            def f():
                res = cols(pair * MXU_COLS) * (HEAD_K ** -0.5)
                q_scr[2 * pair, rows, :] = res[:, :HEAD_K]
                q_scr[2 * pair + 1, rows, :] = res[:, HEAD_K:]
            return f

        def k_item(pair):
            def f():
                res = cols(DK_TOT + pair * MXU_COLS)
                k_scr[2 * pair, rows, :] = res[:, :HEAD_K]
                k_scr[2 * pair + 1, rows, :] = res[:, HEAD_K:]
            return f

        def v_item(n):
            def f():
                v_scr[rows, n * MXU_COLS:(n + 1) * MXU_COLS] = cols(
                    2 * DK_TOT + n * MXU_COLS).astype(BF16)
            return f

        def r_item(n):
            def f():
                r_scr[rows, n * MXU_COLS:(n + 1) * MXU_COLS] = _silu(cols(
                    2 * DK_TOT + DV_TOT + n * MXU_COLS)).astype(BF16)
            return f

        def gate():
            hf = _dot(h_scr[rows, :], wf1_ref[...]).astype(BF16)
            xg = _dot(hf, wf2_ref[...]) + bf_ref[...]
            g_scr[rows, :] = ((jnp.minimum(xg, 0.0) - jnp.log1p(jnp.exp(-jnp.abs(xg))))
                              * (LOG2_E / GATE_NORM))

        items = [norm, gate]
        items += [q_item(p) for p in range(DK_TOT // MXU_COLS)]
        items += [k_item(p) for p in range(DK_TOT // MXU_COLS)]
        for n in range(DV_TOT // MXU_COLS):
            items += [v_item(n), r_item(n)]
        return items

    def out_items(rows):
        def o_item(n):
            def f():
                y_scr[rows, n * MXU_COLS:(n + 1) * MXU_COLS] = _dot(og_scr[rows, :], wout_ref[n])
            return f

        def post():
            y_ref[0, rows, :] = x_ref[0, rows, :] + _rms(y_scr[rows, :], gpost_ref[...])

        return [o_item(n) for n in range(D_MODEL // MXU_COLS)] + [post]

    def run_chunk(c, queue):
        if isinstance(c, int):
            base = c * chunk
            rows = slice(base, base + chunk)
        else:
            base = pl.multiple_of(c * chunk, chunk)
            rows = pl.ds(base, chunk)
        g = g_scr[rows, :]
        g_hi = g.astype(BF16)
        g_lo = (g - g_hi.astype(F32)).astype(BF16)
        bcum = _dot(tri_ref[...], jnp.concatenate([g_hi, g_lo], axis=0))
        level = level_ref[...]
        queue.tick()
        n_blocks = chunk // SUBLANES
        for hd in range(N_HEADS):
            ks = slice(hd * HEAD_K, (hd + 1) * HEAD_K)
            vs = slice(hd * HEAD_V, (hd + 1) * HEAD_V)
            q = q_scr[hd, rows, :]
            k = k_scr[hd, rows, :]
            b = bcum[:, ks]
            bc_scr[hd] = b
            b_blk = [b[r0:r0 + SUBLANES] for r0 in range(0, chunk, SUBLANES)]
            b_last = _row_bcast(bc_scr, hd, chunk - 1)
            q_in = (q * jnp.exp2(b)).astype(BF16)
            k_st = (k * jnp.exp2(jnp.concatenate([b_last - bb for bb in b_blk], axis=0))).astype(BF16)
            d_last = jnp.exp2(b_last[0:1])

            att = jnp.zeros((chunk, chunk), F32)
            for li in range(nlev):
                half = chunk >> (li + 1)
                src, expo = [], []
                for bi in range(n_blocks):
                    r0 = bi * SUBLANES
                    start = r0 - r0 % (2 * half)
                    b_mid = _row_bcast(bc_scr, hd, start + half - 1)
                    if r0 - start >= half:
                        src.append(q[r0:r0 + SUBLANES])
                        expo.append(b_blk[bi] - b_mid)
                    else:
                        src.append(k[r0:r0 + SUBLANES])
                        expo.append(b_mid - b_blk[bi])
                w = (jnp.concatenate(src, axis=0)
                     * jnp.exp2(jnp.concatenate(expo, axis=0))).astype(BF16)
                att = jnp.where(level == li,
                                lax.dot_general(w, w, nt, preferred_element_type=F32), att)
            queue.tick()

            blocks = []
            for bi in range(n_blocks):
                r0 = bi * SUBLANES
                qb = q[r0:r0 + SUBLANES]
                bb = b_blk[bi]
                blk = jnp.zeros((SUBLANES, LANES), F32)
                for j in range(SUBLANES):
                    k_row = _row_bcast(k_scr, hd, base + r0 + j)
                    b_row = _row_bcast(bc_scr, hd, r0 + j)
                    p = qb * (k_row * jnp.exp2(bb - b_row))
                    blk = jnp.where(lane == j, jnp.sum(p, axis=1, keepdims=True), blk)
                blocks.append(blk)
                if bi % DIAG_BLOCKS_PER_TICK == DIAG_BLOCKS_PER_TICK - 1:
                    queue.tick()
            a_diag = _dot(jnp.concatenate(blocks, axis=0).astype(BF16), spread_ref[...])
            att = jnp.where(level == nlev, a_diag, att)
            queue.tick()

            v = v_scr[rows, vs]
            s = s_scr[hd]
            if chunk % LANES == 0:
                o = _dot(jnp.concatenate([q_in, att.astype(BF16)], axis=1),
                         jnp.concatenate([s.astype(BF16), v], axis=0))
            else:
                o = _dot(q_in, s.astype(BF16)) + _dot(att.astype(BF16), v)
            on = _rms(o, ghead_ref[:, vs])
            og_scr[rows, vs] = (on * r_scr[rows, vs].astype(F32)).astype(BF16)
            d_col = jnp.sum(jnp.where(eye, d_last, 0.0), axis=1, keepdims=True)
            kv = lax.dot_general(k_st, v, tn, preferred_element_type=F32)
            s_scr[hd] = d_col * s + kv

    def run_phase(chunks, items):
        ticks_per_head = 2 + (chunk // SUBLANES) // DIAG_BLOCKS_PER_TICK
        queue = _WorkQueue(items, len(chunks) * (1 + N_HEADS * ticks_per_head))
        for c in chunks:
            run_chunk(c, queue)
        queue.flush()

    def blocks(lo, hi, step):
        return [slice(r, r + step) for r in range(lo, hi, step)]

    if pipelined:
        half = tile // 2
        first, second = blocks(0, half, ROW_BLOCK), blocks(half, tile, ROW_BLOCK)

        @pl.when(i == 0)
        def _():
            s_scr[...] = s0_ref[0]
            for rows in first:
                for item in proj_items(x_ref, rows, rows):
                    item()

        run_phase(range(0, n_chunks // 2),
                  [it for rows in second for it in proj_items(x_ref, rows, rows)])
        run_phase(range(n_chunks // 2, n_chunks),
                  [it for rows in first for it in proj_items(xn_ref, rows, rows)]
                  + [it for rows in first for it in out_items(rows)])
        for rows in second:
            for item in out_items(rows):
                item()
        @pl.when(i == pl.num_programs(1) - 1)
        def _():
            sout_ref[0] = s_scr[...]
    else:
        whole = slice(0, tile)

        @pl.when(i == 0)
        def _():
            for item in proj_items(x_ref, whole, whole):
                item()

        s_scr[...] = s0_ref[0]
        run_phase([i], [])
        sout_ref[0] = s_scr[...]

        @pl.when(i == pl.num_programs(1) - 1)
        def _():
            for item in out_items(whole):
                item()


def _conv_kernel(x_ref, hist_ref, gpre_ref, gpost_ref, win_ref, cw_ref, wout_ref,
                 y_ref, hout_ref,
                 xc_scr,
                 *, tile, seg):
    i = pl.program_id(1)

    if seg is None:
        @pl.when(i == 0)
        def _():
            xc_scr[...] = hist_ref[0]
        top = lax.broadcasted_iota(jnp.int32, (HIST_ROW, CONV_COLS), 0)
    else:
        pos = lax.broadcasted_iota(jnp.int32, (tile, CONV_COLS), 0) & (seg - 1)

    x = x_ref[0]
    h = _rms(x, gpre_ref[...]).astype(BF16)
    n_out = D_MODEL // MXU_COLS
    acc = [jnp.zeros((tile, MXU_COLS), F32) for _ in range(n_out)]
    n_cb = D_INNER // CONV_COLS
    for cb in range(n_cb):
        cols = slice(cb * CONV_COLS, (cb + 1) * CONV_COLS)

        def proj(part):
            return _dot(h, win_ref[part * n_cb + cb])

        xc = proj(1) * proj(2)
        x1 = pltpu.roll(xc, 1, 0)
        x2 = pltpu.roll(xc, 2, 0)
        if seg is None:
            hist = xc_scr[:, cols]
            x1 = jnp.concatenate([jnp.where(top < 1, pltpu.roll(hist, 1, 0), x1[0:HIST_ROW]),
                                  x1[HIST_ROW:]], axis=0)
            x2 = jnp.concatenate([jnp.where(top < 2, pltpu.roll(hist, 2, 0), x2[0:HIST_ROW]),
                                  x2[HIST_ROW:]], axis=0)
        else:
            x1 = jnp.where(pos < 1, hist_ref[0, 0, :, cols], x1)
            x2 = jnp.where(pos < 2, hist_ref[0, 1, :, cols], x2)
            hout_ref[0, :, cols] = xc
        conv = x2 * cw_ref[0:1, cols] + x1 * cw_ref[1:2, cols] + xc * cw_ref[2:3, cols]
        yb = (proj(0) * conv * _silu(proj(3))).astype(BF16)
        for n in range(n_out):
            acc[n] = acc[n] + _dot(yb, wout_ref[n, cols, :])
        if seg is None:
            xc_scr[:, cols] = xc[tile - HIST_ROW:tile]
    y_ref[0] = x + _rms(jnp.concatenate(acc, axis=1), gpost_ref[...])

    if seg is None:
        @pl.when(i == pl.num_programs(1) - 1)
        def _():
            hout_ref[0] = xc_scr[...]


def _tile_sizes(t_len):
    tile = min(t_len, MAX_TILE)
    chunk = min(tile, MAX_CHUNK)
    assert t_len % tile == 0 and tile % chunk == 0 and chunk % SUBLANES == 0
    return tile, chunk


def _const_spec(shape):
    return pl.BlockSpec(shape, lambda b, i: (0,) * len(shape), pipeline_mode=pl.Buffered(1))


def _layer_spec(stacked, li):
    rest = stacked.shape[1:]
    return pl.BlockSpec((None,) + rest, lambda b, i: (li,) + (0,) * len(rest),
                        pipeline_mode=pl.Buffered(1))


def _cast_kernel(w_ref, o_ref):
    o_ref[0, 0] = w_ref[0].astype(BF16)


def _col_pieces(w, width):
    n_layers, k_dim, n_dim = w.shape
    return pl.pallas_call(
        _cast_kernel,
        grid=(n_layers, n_dim // width),
        in_specs=[pl.BlockSpec((1, k_dim, width), lambda l, n: (l, 0, n))],
        out_specs=pl.BlockSpec((1, 1, k_dim, width), lambda l, n: (l, n, 0, 0)),
        out_shape=jax.ShapeDtypeStruct((n_layers, n_dim // width, k_dim, width), BF16),
        compiler_params=pltpu.CompilerParams(dimension_semantics=("arbitrary", "arbitrary")),
        name="weight_pieces",
    )(w)


def _gla_layer(x, s0, s0_li, li, gpre, gpost, win, wf1, wf2, bf, ghead, wout):
    bsz, t_len, _ = x.shape
    batched = t_len <= MAX_CHUNK and bsz * t_len <= MAX_TILE
    if batched:
        tile, chunk = bsz * t_len, t_len
        x_in = x.reshape(1, tile, D_MODEL)
        grid = (1, bsz)
        tok_spec = pl.BlockSpec((1, tile, D_MODEL), lambda b, i: (0, 0, 0))
        st_spec = pl.BlockSpec((1, N_HEADS, HEAD_K, HEAD_V), lambda b, i: (i, 0, 0, 0))
        st_in_spec = pl.BlockSpec((None, 1, N_HEADS, HEAD_K, HEAD_V),
                                  lambda b, i: (s0_li, i, 0, 0, 0))
        n_tiles = 1
    else:
        tile, chunk = _tile_sizes(t_len)
        n_tiles = t_len // tile
        assert (tile // chunk) % 2 == 0 and (tile // 2) % ROW_BLOCK == 0
        x_in = x
        grid = (bsz, n_tiles)
        tok_spec = pl.BlockSpec((1, tile, D_MODEL), lambda b, i: (b, i, 0))
        st_spec = pl.BlockSpec((1, N_HEADS, HEAD_K, HEAD_V), lambda b, i: (b, 0, 0, 0))
        st_in_spec = pl.BlockSpec((None, 1, N_HEADS, HEAD_K, HEAD_V),
                                  lambda b, i: (s0_li, b, 0, 0, 0))
    nlev, tri, level, spread = _chunk_constants(chunk)
    half = tile // 2
    kern = functools.partial(_gla_kernel, tile=tile, chunk=chunk, nlev=nlev,
                             pipelined=not batched)
    next_spec = pl.BlockSpec(
        (1, half, D_MODEL),
        lambda b, i: (0 if batched else b, jnp.minimum(2 * i + 2, 2 * n_tiles - 1), 0))
    y, s_out = pl.pallas_call(
        kern,
        grid=grid,
        in_specs=[
            tok_spec, next_spec, st_in_spec,
            _const_spec((1, D_MODEL)), _const_spec((1, D_MODEL)),
            _layer_spec(win, li),
            _const_spec((D_MODEL, LANES)), _const_spec((LANES, DK_TOT)),
            _const_spec((1, DK_TOT)), _const_spec((1, DV_TOT)),
            _layer_spec(wout, li),
            _const_spec(tri.shape), _const_spec((chunk, chunk)),
            _const_spec((LANES, chunk)),
        ],
        out_specs=[tok_spec, st_spec],
        out_shape=[jax.ShapeDtypeStruct(x_in.shape, F32),
                   jax.ShapeDtypeStruct(s0.shape[1:], F32)],
        scratch_shapes=[
            pltpu.VMEM((N_HEADS, HEAD_K, HEAD_V), F32),
            pltpu.VMEM((tile, D_MODEL), BF16),
            pltpu.VMEM((N_HEADS, tile, HEAD_K), F32),
            pltpu.VMEM((N_HEADS, tile, HEAD_K), F32),
            pltpu.VMEM((tile, DK_TOT), F32),
            pltpu.VMEM((tile, DV_TOT), BF16),
            pltpu.VMEM((tile, DV_TOT), BF16),
            pltpu.VMEM((tile, DV_TOT), BF16),
            pltpu.VMEM((N_HEADS, chunk, HEAD_K), F32),
            pltpu.VMEM((tile, D_MODEL), F32),
        ],
        compiler_params=pltpu.CompilerParams(
            dimension_semantics=("arbitrary", "arbitrary"),
            vmem_limit_bytes=VMEM_LIMIT_BYTES),
        name="gla_layer",
    )(x_in, x_in, s0, gpre, gpost, win, wf1, wf2, bf, ghead, wout, tri, level, spread)
    return y.reshape(x.shape), s_out


def _conv_layer(x, hist, li, gpre, gpost, win, cw, wout):
    bsz, t_len, _ = x.shape
    n_hist = CONV_W - 1
    segmented = bsz * t_len <= MAX_TILE and t_len & (t_len - 1) == 0 and t_len >= n_hist
    if segmented:
        tile = bsz * t_len
        taps = [jnp.pad(hist[:, n_hist - d:], ((0, 0), (0, t_len - d), (0, 0))).reshape(tile, D_INNER)
                for d in range(1, CONV_W)]
        hist_in = jnp.stack(taps)[None]
        x_in = x.reshape(1, tile, D_MODEL)
        grid = (1, 1)
        hist_spec = pl.BlockSpec((1, n_hist, tile, D_INNER), lambda b, i: (0, 0, 0, 0))
        hout_spec = pl.BlockSpec((1, tile, D_INNER), lambda b, i: (0, 0, 0))
        hout_shape = (1, tile, D_INNER)
        seg = t_len
    else:
        tile, _ = _tile_sizes(t_len)
        assert tile >= HIST_ROW
        hist_in = jnp.pad(hist, ((0, 0), (HIST_ROW - n_hist, 0), (0, 0)))
        x_in = x
        grid = (bsz, t_len // tile)
        hist_spec = hout_spec = pl.BlockSpec((1, HIST_ROW, D_INNER), lambda b, i: (b, 0, 0))
        hout_shape = (bsz, HIST_ROW, D_INNER)
        seg = None
    kern = functools.partial(_conv_kernel, tile=tile, seg=seg)
    tok_spec = pl.BlockSpec((1, tile, D_MODEL), lambda b, i: (b, i, 0))
    y, hout = pl.pallas_call(
        kern,
        grid=grid,
        in_specs=[
            tok_spec, hist_spec,
            _const_spec((1, D_MODEL)), _const_spec((1, D_MODEL)),
            _layer_spec(win, li),
            _const_spec((CONV_W, D_INNER)),
            _layer_spec(wout, li),
        ],
        out_specs=[tok_spec, hout_spec],
        out_shape=[jax.ShapeDtypeStruct(x_in.shape, F32),
                   jax.ShapeDtypeStruct(hout_shape, F32)],
        scratch_shapes=[pltpu.VMEM((HIST_ROW, D_INNER), F32)],
        compiler_params=pltpu.CompilerParams(
            dimension_semantics=("arbitrary", "arbitrary"),
            vmem_limit_bytes=VMEM_LIMIT_BYTES),
        name="conv_layer",
    )(x_in, hist_in, gpre, gpost, win, cw, wout)
    if segmented:
        return (y.reshape(x.shape),
                hout.reshape(bsz, t_len, D_INNER)[:, t_len - n_hist:])
    return y, hout[:, HIST_ROW - n_hist:]


def kernel(x_prompt, x_sample, state_gla, state_conv, norm_pre, norm_post, gla_w_in, gla_w_f1, gla_w_f2, gla_b_f, gla_g_head, gla_w_out, conv_w_in, conv_w, conv_w_out):
    depth = norm_pre.shape[0]
    xp, xs = x_prompt, x_sample
    s0_prompt = jnp.zeros((1, xp.shape[0], N_HEADS, HEAD_K, HEAD_V), F32)
    hist_prompt = jnp.zeros((xp.shape[0], CONV_W - 1, D_INNER), F32)
    gla_p, gla_s, conv_p, conv_s = [], [], [], []
    gla_win = _col_pieces(gla_w_in, MXU_COLS)
    gla_wout = _col_pieces(gla_w_out, MXU_COLS)
    conv_win = _col_pieces(conv_w_in, CONV_COLS)
    conv_wout = _col_pieces(conv_w_out, MXU_COLS)
    for layer in range(depth):
        li = layer // 2
        gpre = norm_pre[layer][None]
        gpost = norm_post[layer][None]
        if layer % 2 == 0:
            rank = gla_w_f1.shape[-1]
            prm = (gpre, gpost,
                   gla_win,
                   jnp.pad(gla_w_f1[li], ((0, 0), (0, LANES - rank))).astype(BF16),
                   jnp.pad(gla_w_f2[li], ((0, LANES - rank), (0, 0))).astype(BF16),
                   gla_b_f[li][None], gla_g_head[li][None],
                   gla_wout)
            xp, sp = _gla_layer(xp, s0_prompt, 0, li, *prm)
            xs, ss = _gla_layer(xs, state_gla, li, li, *prm)
            gla_p.append(sp)
            gla_s.append(ss)
        else:
            prm = (gpre, gpost, conv_win, conv_w[li], conv_wout)
            xp, sp = _conv_layer(xp, hist_prompt, li, *prm)
            xs, ss = _conv_layer(xs, state_conv[li], li, *prm)
            conv_p.append(sp)
            conv_s.append(ss)
    return (xp, xs, jnp.stack(gla_p), jnp.stack(conv_p), jnp.stack(gla_s), jnp.stack(conv_s))
```

```python
import functools

import numpy as np
import jax
import jax.numpy as jnp
from jax import lax
from jax.experimental import pallas as pl
from jax.experimental.pallas import tpu as pltpu

D_MODEL = 1024
N_HEADS = 4
HEAD_K = 128
HEAD_V = 512
DK_TOT = N_HEADS * HEAD_K
DV_TOT = N_HEADS * HEAD_V
D_INNER = 2 * D_MODEL
GATE_NORM = 16.0
CONV_W = 3
EPS = 1e-6
LOG2_E = 1.4426950408889634

SUBLANES = 8
LANES = 128
MXU_COLS = 256
VMEM_LIMIT_BYTES = 56 * 1024 * 1024

MAX_TILE = 512
MAX_CHUNK = 128
ROW_BLOCK = 256
DIAG_BLOCKS_PER_TICK = 2
CONV_COLS = 1024
CONV_PIECE = 512
HIST_ROW = SUBLANES

F32 = jnp.float32
BF16 = jnp.bfloat16


def _dot(a, b):
    return jnp.dot(a, b, preferred_element_type=F32)


def _rms(x, g):
    return x * lax.rsqrt(jnp.mean(x * x, axis=-1, keepdims=True) + EPS) * g


def _silu(x):
    return x / (1.0 + jnp.exp(-x))


def _row_bcast(ref, lead, r):
    return ref[lead, pl.ds(r, SUBLANES, stride=0), :]


def _chunk_constants(chunk):
    nlev = int(np.log2(chunk // SUBLANES))
    assert SUBLANES << nlev == chunk
    r = np.arange(chunk)[:, None]
    u = np.arange(chunk)[None, :]
    tri = (u <= r).astype(np.float32)
    tri = np.concatenate([tri, tri], axis=1)
    level = np.full((chunk, chunk), -1, np.int32)
    for li in range(nlev):
        half = chunk >> (li + 1)
        blk = 2 * half
        level[(r // blk == u // blk) & ((r % blk) >= half) & ((u % blk) < half)] = li
    level[(r // SUBLANES == u // SUBLANES) & (u <= r)] = nlev
    lane = np.arange(LANES)[:, None]
    spread = ((lane < SUBLANES) & (u % SUBLANES == lane)).astype(np.float32)
    return nlev, jnp.asarray(tri, BF16), jnp.asarray(level), jnp.asarray(spread, BF16)


class _WorkQueue:
    def __init__(self, items, n_ticks):
        self.items = list(items)
        self.total = len(self.items)
        self.n_ticks = max(n_ticks, 1)
        self.done = 0
        self.ticks = 0

    def tick(self):
        self.ticks += 1
        target = min(self.total, -(-self.total * self.ticks // self.n_ticks))
        while self.done < target:
            self.items[self.done]()
            self.done += 1

    def flush(self):
        while self.done < self.total:
            self.items[self.done]()
            self.done += 1


def _gla_kernel(x_ref, xn_ref, s0_ref, gpre_ref, gpost_ref, win_ref, wf1_ref, wf2_ref, bf_ref,
                ghead_ref, wout_ref, tri_ref, level_ref, spread_ref,
                y_ref, sout_ref,
                s_scr, h_scr, q_scr, k_scr, g_scr, v_scr, r_scr, og_scr, bc_scr, y_scr,
                *, tile, chunk, nlev, pipelined):
    i = pl.program_id(1)
    n_chunks = tile // chunk
    lane = lax.broadcasted_iota(jnp.int32, (SUBLANES, LANES), 1)
    eye = (lax.broadcasted_iota(jnp.int32, (HEAD_K, HEAD_K), 0)
           == lax.broadcasted_iota(jnp.int32, (HEAD_K, HEAD_K), 1))
    nt = (((1,), (1,)), ((), ()))
    tn = (((0,), (0,)), ((), ()))

    def proj_items(src_ref, src_rows, rows):
        def norm():
            h_scr[rows, :] = _rms(src_ref[0, src_rows, :], gpre_ref[...]).astype(BF16)

        def cols(lo):
            return _dot(h_scr[rows, :], win_ref[lo // MXU_COLS])

        def q_item(pair):
            def f():
                res = cols(pair * MXU_COLS) * (HEAD_K ** -0.5)
                q_scr[2 * pair, rows, :] = res[:, :HEAD_K]
                q_scr[2 * pair + 1, rows, :] = res[:, HEAD_K:]
            return f

        def k_item(pair):
            def f():
                res = cols(DK_TOT + pair * MXU_COLS)
                k_scr[2 * pair, rows, :] = res[:, :HEAD_K]
                k_scr[2 * pair + 1, rows, :] = res[:, HEAD_K:]
            return f

        def v_item(n):
            def f():
                v_scr[rows, n * MXU_COLS:(n + 1) * MXU_COLS] = cols(
                    2 * DK_TOT + n * MXU_COLS).astype(BF16)
            return f

        def r_item(n):
            def f():
                r_scr[rows, n * MXU_COLS:(n + 1) * MXU_COLS] = _silu(cols(
                    2 * DK_TOT + DV_TOT + n * MXU_COLS)).astype(BF16)
            return f

        def gate():
            hf = _dot(h_scr[rows, :], wf1_ref[...]).astype(BF16)
            xg = _dot(hf, wf2_ref[...]) + bf_ref[...]
            g_scr[rows, :] = ((jnp.minimum(xg, 0.0) - jnp.log1p(jnp.exp(-jnp.abs(xg))))
                              * (LOG2_E / GATE_NORM))

        items = [norm, gate]
        items += [q_item(p) for p in range(DK_TOT // MXU_COLS)]
        items += [k_item(p) for p in range(DK_TOT // MXU_COLS)]
        for n in range(DV_TOT // MXU_COLS):
            items += [v_item(n), r_item(n)]
        return items

    def out_items(rows):
        def o_item(n):
            def f():
                y_scr[rows, n * MXU_COLS:(n + 1) * MXU_COLS] = _dot(og_scr[rows, :], wout_ref[n])
            return f

        def post():
            y_ref[0, rows, :] = x_ref[0, rows, :] + _rms(y_scr[rows, :], gpost_ref[...])

        return [o_item(n) for n in range(D_MODEL // MXU_COLS)] + [post]

    def run_chunk(c, queue):
        if isinstance(c, int):
            base = c * chunk
            rows = slice(base, base + chunk)
        else:
            base = pl.multiple_of(c * chunk, chunk)
            rows = pl.ds(base, chunk)
        g = g_scr[rows, :]
        g_hi = g.astype(BF16)
        g_lo = (g - g_hi.astype(F32)).astype(BF16)
        bcum = _dot(tri_ref[...], jnp.concatenate([g_hi, g_lo], axis=0))
        level = level_ref[...]
        queue.tick()
        n_blocks = chunk // SUBLANES
        for hd in range(N_HEADS):
            ks = slice(hd * HEAD_K, (hd + 1) * HEAD_K)
            vs = slice(hd * HEAD_V, (hd + 1) * HEAD_V)
            q = q_scr[hd, rows, :]
            k = k_scr[hd, rows, :]
            b = bcum[:, ks]
            bc_scr[hd] = b
            b_blk = [b[r0:r0 + SUBLANES] for r0 in range(0, chunk, SUBLANES)]
            b_last = _row_bcast(bc_scr, hd, chunk - 1)
            q_in = (q * jnp.exp2(b)).astype(BF16)
            k_st = (k * jnp.exp2(jnp.concatenate([b_last - bb for bb in b_blk], axis=0))).astype(BF16)
            d_last = jnp.exp2(b_last[0:1])

            att = jnp.zeros((chunk, chunk), F32)
            for li in range(nlev):
                half = chunk >> (li + 1)
                src, expo = [], []
                for bi in range(n_blocks):
                    r0 = bi * SUBLANES
                    start = r0 - r0 % (2 * half)
                    b_mid = _row_bcast(bc_scr, hd, start + half - 1)
                    if r0 - start >= half:
                        src.append(q[r0:r0 + SUBLANES])
                        expo.append(b_blk[bi] - b_mid)
                    else:
                        src.append(k[r0:r0 + SUBLANES])
                        expo.append(b_mid - b_blk[bi])
                w = (jnp.concatenate(src, axis=0)
                     * jnp.exp2(jnp.concatenate(expo, axis=0))).astype(BF16)
                att = jnp.where(level == li,
                                lax.dot_general(w, w, nt, preferred_element_type=F32), att)
            queue.tick()

            blocks = []
            for bi in range(n_blocks):
                r0 = bi * SUBLANES
                qb = q[r0:r0 + SUBLANES]
                bb = b_blk[bi]
                blk = jnp.zeros((SUBLANES, LANES), F32)
                for j in range(SUBLANES):
                    k_row = _row_bcast(k_scr, hd, base + r0 + j)
                    b_row = _row_bcast(bc_scr, hd, r0 + j)
                    p = qb * (k_row * jnp.exp2(bb - b_row))
                    blk = jnp.where(lane == j, jnp.sum(p, axis=1, keepdims=True), blk)
                blocks.append(blk)
                if bi % DIAG_BLOCKS_PER_TICK == DIAG_BLOCKS_PER_TICK - 1:
                    queue.tick()
            a_diag = _dot(jnp.concatenate(blocks, axis=0).astype(BF16), spread_ref[...])
            att = jnp.where(level == nlev, a_diag, att)
            queue.tick()

            v = v_scr[rows, vs]
            s = s_scr[hd]
            if chunk % LANES == 0:
                o = _dot(jnp.concatenate([q_in, att.astype(BF16)], axis=1),
                         jnp.concatenate([s.astype(BF16), v], axis=0))
            else:
                o = _dot(q_in, s.astype(BF16)) + _dot(att.astype(BF16), v)
            on = _rms(o, ghead_ref[:, vs])
            og_scr[rows, vs] = (on * r_scr[rows, vs].astype(F32)).astype(BF16)
            d_col = jnp.sum(jnp.where(eye, d_last, 0.0), axis=1, keepdims=True)
            kv = lax.dot_general(k_st, v, tn, preferred_element_type=F32)
            s_scr[hd] = d_col * s + kv

    def run_phase(chunks, items):
        ticks_per_head = 2 + (chunk // SUBLANES) // DIAG_BLOCKS_PER_TICK
        queue = _WorkQueue(items, len(chunks) * (1 + N_HEADS * ticks_per_head))
        for c in chunks:
            run_chunk(c, queue)
        queue.flush()

    def blocks(lo, hi, step):
        return [slice(r, r + step) for r in range(lo, hi, step)]

    if pipelined:
        half = tile // 2
        first, second = blocks(0, half, ROW_BLOCK), blocks(half, tile, ROW_BLOCK)

        @pl.when(i == 0)
        def _():
            s_scr[...] = s0_ref[0]
            for rows in first:
                for item in proj_items(x_ref, rows, rows):
                    item()

        run_phase(range(0, n_chunks // 2),
                  [it for rows in second for it in proj_items(x_ref, rows, rows)])
        run_phase(range(n_chunks // 2, n_chunks),
                  [it for rows in first for it in proj_items(xn_ref, rows, rows)]
                  + [it for rows in first for it in out_items(rows)])
        for rows in second:
            for item in out_items(rows):
                item()
        @pl.when(i == pl.num_programs(1) - 1)
        def _():
            sout_ref[0] = s_scr[...]
    else:
        whole = slice(0, tile)

        @pl.when(i == 0)
        def _():
            for item in proj_items(x_ref, whole, whole):
                item()

        s_scr[...] = s0_ref[0]
        run_phase([i], [])
        sout_ref[0] = s_scr[...]

        @pl.when(i == pl.num_programs(1) - 1)
        def _():
            for item in out_items(whole):
                item()


def _conv_kernel(x_ref, hist_ref, gpre_ref, gpost_ref, win_ref, cw_ref, wout_ref,
                 y_ref, hout_ref,
                 xc_scr,
                 *, tile, seg):
    i = pl.program_id(1)

    if seg is None:
        @pl.when(i == 0)
        def _():
            xc_scr[0:HIST_ROW, :] = hist_ref[0]
    else:
        xc_scr[0:HIST_ROW, :] = jnp.zeros((HIST_ROW, D_INNER), F32)
        pos = lax.broadcasted_iota(jnp.int32, (tile, CONV_COLS), 0) & (seg - 1)

    x = x_ref[0]
    h = _rms(x, gpre_ref[...]).astype(BF16)
    n_out = D_MODEL // MXU_COLS
    acc = [jnp.zeros((tile, MXU_COLS), F32) for _ in range(n_out)]
    n_cb = D_INNER // CONV_COLS
    for cb in range(n_cb):
        cols = slice(cb * CONV_COLS, (cb + 1) * CONV_COLS)

        def proj(part):
            n_w = CONV_COLS // CONV_PIECE
            first = (part * n_cb + cb) * n_w
            return jnp.concatenate([_dot(h, win_ref[first + p]) for p in range(n_w)], axis=1)

        xc = proj(1) * proj(2)
        xc_scr[HIST_ROW:HIST_ROW + tile, cols] = xc
        x2 = xc_scr[HIST_ROW - 2:HIST_ROW - 2 + tile, cols]
        x1 = xc_scr[HIST_ROW - 1:HIST_ROW - 1 + tile, cols]
        if seg is not None:
            x1 = jnp.where(pos < 1, hist_ref[0, 0, :, cols], x1)
            x2 = jnp.where(pos < 2, hist_ref[0, 1, :, cols], x2)
            hout_ref[0, :, cols] = xc
        conv = x2 * cw_ref[0:1, cols] + x1 * cw_ref[1:2, cols] + xc * cw_ref[2:3, cols]
        yb = (proj(0) * conv * _silu(proj(3))).astype(BF16)
        for n in range(n_out):
            acc[n] = acc[n] + _dot(yb, wout_ref[n, cols, :])
        if seg is None:
            xc_scr[0:HIST_ROW, cols] = xc[tile - HIST_ROW:tile]
    y_ref[0] = x + _rms(jnp.concatenate(acc, axis=1), gpost_ref[...])

    if seg is None:
        @pl.when(i == pl.num_programs(1) - 1)
        def _():
            hout_ref[0] = xc_scr[0:HIST_ROW, :]


def _tile_sizes(t_len):
    tile = min(t_len, MAX_TILE)
    chunk = min(tile, MAX_CHUNK)
    assert t_len % tile == 0 and tile % chunk == 0 and chunk % SUBLANES == 0
    return tile, chunk


def _const_spec(shape):
    return pl.BlockSpec(shape, lambda b, i: (0,) * len(shape), pipeline_mode=pl.Buffered(1))


def _layer_spec(stacked, li):
    rest = stacked.shape[1:]
    return pl.BlockSpec((None,) + rest, lambda b, i: (li,) + (0,) * len(rest),
                        pipeline_mode=pl.Buffered(1))


def _cast_kernel(w_ref, o_ref):
    o_ref[0, 0] = w_ref[0].astype(BF16)


def _col_pieces(w, width):
    n_layers, k_dim, n_dim = w.shape
    return pl.pallas_call(
        _cast_kernel,
        grid=(n_layers, n_dim // width),
        in_specs=[pl.BlockSpec((1, k_dim, width), lambda l, n: (l, 0, n))],
        out_specs=pl.BlockSpec((1, 1, k_dim, width), lambda l, n: (l, n, 0, 0)),
        out_shape=jax.ShapeDtypeStruct((n_layers, n_dim // width, k_dim, width), BF16),
        compiler_params=pltpu.CompilerParams(dimension_semantics=("arbitrary", "arbitrary")),
        name="weight_pieces",
    )(w)


def _gla_layer(x, s0, s0_li, li, gpre, gpost, win, wf1, wf2, bf, ghead, wout):
    bsz, t_len, _ = x.shape
    batched = t_len <= MAX_CHUNK and bsz * t_len <= MAX_TILE
    if batched:
        tile, chunk = bsz * t_len, t_len
        x_in = x.reshape(1, tile, D_MODEL)
        grid = (1, bsz)
        tok_spec = pl.BlockSpec((1, tile, D_MODEL), lambda b, i: (0, 0, 0))
        st_spec = pl.BlockSpec((1, N_HEADS, HEAD_K, HEAD_V), lambda b, i: (i, 0, 0, 0))
        st_in_spec = pl.BlockSpec((None, 1, N_HEADS, HEAD_K, HEAD_V),
                                  lambda b, i: (s0_li, i, 0, 0, 0))
        n_tiles = 1
    else:
        tile, chunk = _tile_sizes(t_len)
        n_tiles = t_len // tile
        assert (tile // chunk) % 2 == 0 and (tile // 2) % ROW_BLOCK == 0
        x_in = x
        grid = (bsz, n_tiles)
        tok_spec = pl.BlockSpec((1, tile, D_MODEL), lambda b, i: (b, i, 0))
        st_spec = pl.BlockSpec((1, N_HEADS, HEAD_K, HEAD_V), lambda b, i: (b, 0, 0, 0))
        st_in_spec = pl.BlockSpec((None, 1, N_HEADS, HEAD_K, HEAD_V),
                                  lambda b, i: (s0_li, b, 0, 0, 0))
    nlev, tri, level, spread = _chunk_constants(chunk)
    half = tile // 2
    kern = functools.partial(_gla_kernel, tile=tile, chunk=chunk, nlev=nlev,
                             pipelined=not batched)
    next_spec = pl.BlockSpec(
        (1, half, D_MODEL),
        lambda b, i: (0 if batched else b, jnp.minimum(2 * i + 2, 2 * n_tiles - 1), 0))
    y, s_out = pl.pallas_call(
        kern,
        grid=grid,
        in_specs=[
            tok_spec, next_spec, st_in_spec,
            _const_spec((1, D_MODEL)), _const_spec((1, D_MODEL)),
            _layer_spec(win, li),
            _const_spec((D_MODEL, LANES)), _const_spec((LANES, DK_TOT)),
            _const_spec((1, DK_TOT)), _const_spec((1, DV_TOT)),
            _layer_spec(wout, li),
            _const_spec(tri.shape), _const_spec((chunk, chunk)),
            _const_spec((LANES, chunk)),
        ],
        out_specs=[tok_spec, st_spec],
        out_shape=[jax.ShapeDtypeStruct(x_in.shape, F32),
                   jax.ShapeDtypeStruct(s0.shape[1:], F32)],
        scratch_shapes=[
            pltpu.VMEM((N_HEADS, HEAD_K, HEAD_V), F32),
            pltpu.VMEM((tile, D_MODEL), BF16),
            pltpu.VMEM((N_HEADS, tile, HEAD_K), F32),
            pltpu.VMEM((N_HEADS, tile, HEAD_K), F32),
            pltpu.VMEM((tile, DK_TOT), F32),
            pltpu.VMEM((tile, DV_TOT), BF16),
            pltpu.VMEM((tile, DV_TOT), BF16),
            pltpu.VMEM((tile, DV_TOT), BF16),
            pltpu.VMEM((N_HEADS, chunk, HEAD_K), F32),
            pltpu.VMEM((tile, D_MODEL), F32),
        ],
        compiler_params=pltpu.CompilerParams(
            dimension_semantics=("arbitrary", "arbitrary"),
            vmem_limit_bytes=VMEM_LIMIT_BYTES),
        name="gla_layer",
    )(x_in, x_in, s0, gpre, gpost, win, wf1, wf2, bf, ghead, wout, tri, level, spread)
    return y.reshape(x.shape), s_out


def _conv_layer(x, hist, li, gpre, gpost, win, cw, wout):
    bsz, t_len, _ = x.shape
    n_hist = CONV_W - 1
    segmented = bsz * t_len <= MAX_TILE and t_len & (t_len - 1) == 0 and t_len >= n_hist
    if segmented:
        tile = bsz * t_len
        taps = [jnp.pad(hist[:, n_hist - d:], ((0, 0), (0, t_len - d), (0, 0))).reshape(tile, D_INNER)
                for d in range(1, CONV_W)]
        hist_in = jnp.stack(taps)[None]
        x_in = x.reshape(1, tile, D_MODEL)
        grid = (1, 1)
        hist_spec = pl.BlockSpec((1, n_hist, tile, D_INNER), lambda b, i: (0, 0, 0, 0))
        hout_spec = pl.BlockSpec((1, tile, D_INNER), lambda b, i: (0, 0, 0))
        hout_shape = (1, tile, D_INNER)
        seg = t_len
    else:
        tile, _ = _tile_sizes(t_len)
        assert tile >= HIST_ROW
        hist_in = jnp.pad(hist, ((0, 0), (HIST_ROW - n_hist, 0), (0, 0)))
        x_in = x
        grid = (bsz, t_len // tile)
        hist_spec = hout_spec = pl.BlockSpec((1, HIST_ROW, D_INNER), lambda b, i: (b, 0, 0))
        hout_shape = (bsz, HIST_ROW, D_INNER)
        seg = None
    kern = functools.partial(_conv_kernel, tile=tile, seg=seg)
    tok_spec = pl.BlockSpec((1, tile, D_MODEL), lambda b, i: (b, i, 0))
    y, hout = pl.pallas_call(
        kern,
        grid=grid,
        in_specs=[
            tok_spec, hist_spec,
            _const_spec((1, D_MODEL)), _const_spec((1, D_MODEL)),
            _layer_spec(win, li),
            _const_spec((CONV_W, D_INNER)),
            _layer_spec(wout, li),
        ],
        out_specs=[tok_spec, hout_spec],
        out_shape=[jax.ShapeDtypeStruct(x_in.shape, F32),
                   jax.ShapeDtypeStruct(hout_shape, F32)],
        scratch_shapes=[pltpu.VMEM((HIST_ROW + tile, D_INNER), F32)],
        compiler_params=pltpu.CompilerParams(
            dimension_semantics=("arbitrary", "arbitrary"),
            vmem_limit_bytes=VMEM_LIMIT_BYTES),
        name="conv_layer",
    )(x_in, hist_in, gpre, gpost, win, cw, wout)
    if segmented:
        return (y.reshape(x.shape),
                hout.reshape(bsz, t_len, D_INNER)[:, t_len - n_hist:])
    return y, hout[:, HIST_ROW - n_hist:]


def kernel(x_prompt, x_sample, state_gla, state_conv, norm_pre, norm_post, gla_w_in, gla_w_f1, gla_w_f2, gla_b_f, gla_g_head, gla_w_out, conv_w_in, conv_w, conv_w_out):
    depth = norm_pre.shape[0]
    xp, xs = x_prompt, x_sample
    s0_prompt = jnp.zeros((1, xp.shape[0], N_HEADS, HEAD_K, HEAD_V), F32)
    hist_prompt = jnp.zeros((xp.shape[0], CONV_W - 1, D_INNER), F32)
    gla_p, gla_s, conv_p, conv_s = [], [], [], []
    gla_win = _col_pieces(gla_w_in, MXU_COLS)
    gla_wout = _col_pieces(gla_w_out, MXU_COLS)
    conv_win = _col_pieces(conv_w_in, CONV_PIECE)
    conv_wout = _col_pieces(conv_w_out, MXU_COLS)
    for layer in range(depth):
        li = layer // 2
        gpre = norm_pre[layer][None]
        gpost = norm_post[layer][None]
        if layer % 2 == 0:
            rank = gla_w_f1.shape[-1]
            prm = (gpre, gpost,
                   gla_win,
                   jnp.pad(gla_w_f1[li], ((0, 0), (0, LANES - rank))).astype(BF16),
                   jnp.pad(gla_w_f2[li], ((0, LANES - rank), (0, 0))).astype(BF16),
                   gla_b_f[li][None], gla_g_head[li][None],
                   gla_wout)
            xp, sp = _gla_layer(xp, s0_prompt, 0, li, *prm)
            xs, ss = _gla_layer(xs, state_gla, li, li, *prm)
            gla_p.append(sp)
            gla_s.append(ss)
        else:
            prm = (gpre, gpost, conv_win, conv_w[li], conv_wout)
            xp, sp = _conv_layer(xp, hist_prompt, li, *prm)
            xs, ss = _conv_layer(xs, state_conv[li], li, *prm)
            conv_p.append(sp)
            conv_s.append(ss)
    return (xp, xs, jnp.stack(gla_p), jnp.stack(conv_p), jnp.stack(gla_s), jnp.stack(conv_s))
```

```python
import functools

import numpy as np
import jax
import jax.numpy as jnp
from jax import lax
from jax.experimental import pallas as pl
from jax.experimental.pallas import tpu as pltpu

D_MODEL = 1024
N_HEADS = 4
HEAD_K = 128
HEAD_V = 512
DK_TOT = N_HEADS * HEAD_K
DV_TOT = N_HEADS * HEAD_V
D_INNER = 2 * D_MODEL
GATE_NORM = 16.0
CONV_W = 3
EPS = 1e-6
LOG2_E = 1.4426950408889634

SUBLANES = 8
LANES = 128
MXU_COLS = 256
VMEM_LIMIT_BYTES = 56 * 1024 * 1024

MAX_TILE = 512
MAX_CHUNK = 128
ROW_BLOCK = 256
DIAG_BLOCKS_PER_TICK = 2
CONV_COLS = 1024
CONV_PIECE = 512
CAST_BLOCK_BYTES = 4 * 1024 * 1024
HIST_ROW = SUBLANES

F32 = jnp.float32
BF16 = jnp.bfloat16


def _dot(a, b):
    return jnp.dot(a, b, preferred_element_type=F32)


def _rms(x, g):
    return x * lax.rsqrt(jnp.mean(x * x, axis=-1, keepdims=True) + EPS) * g


def _silu(x):
    return x / (1.0 + jnp.exp(-x))


def _row_bcast(ref, lead, r):
    return ref[lead, pl.ds(r, SUBLANES, stride=0), :]


def _chunk_constants(chunk):
    nlev = int(np.log2(chunk // SUBLANES))
    assert SUBLANES << nlev == chunk
    r = np.arange(chunk)[:, None]
    u = np.arange(chunk)[None, :]
    tri = (u <= r).astype(np.float32)
    tri = np.concatenate([tri, tri], axis=1)
    level = np.full((chunk, chunk), -1, np.int32)
    for li in range(nlev):
        half = chunk >> (li + 1)
        blk = 2 * half
        level[(r // blk == u // blk) & ((r % blk) >= half) & ((u % blk) < half)] = li
    level[(r // SUBLANES == u // SUBLANES) & (u <= r)] = nlev
    lane = np.arange(LANES)[:, None]
    spread = ((lane < SUBLANES) & (u % SUBLANES == lane)).astype(np.float32)
    return nlev, jnp.asarray(tri, BF16), jnp.asarray(level), jnp.asarray(spread, BF16)


class _WorkQueue:
    def __init__(self, items, n_ticks):
        self.items = list(items)
        self.total = len(self.items)
        self.n_ticks = max(n_ticks, 1)
        self.done = 0
        self.ticks = 0

    def tick(self):
        self.ticks += 1
        target = min(self.total, -(-self.total * self.ticks // self.n_ticks))
        while self.done < target:
            self.items[self.done]()
            self.done += 1

    def flush(self):
        while self.done < self.total:
            self.items[self.done]()
            self.done += 1


def _gla_kernel(x_ref, xn_ref, s0_ref, gpre_ref, gpost_ref, win_ref, wf1_ref, wf2_ref, bf_ref,
                ghead_ref, wout_ref, tri_ref, level_ref, spread_ref,
                y_ref, sout_ref,
                s_scr, h_scr, q_scr, k_scr, g_scr, v_scr, r_scr, og_scr, bc_scr, y_scr,
                *, tile, chunk, nlev, pipelined):
    i = pl.program_id(1)
    n_chunks = tile // chunk
    lane = lax.broadcasted_iota(jnp.int32, (SUBLANES, LANES), 1)
    eye = (lax.broadcasted_iota(jnp.int32, (HEAD_K, HEAD_K), 0)
           == lax.broadcasted_iota(jnp.int32, (HEAD_K, HEAD_K), 1))
    nt = (((1,), (1,)), ((), ()))
    tn = (((0,), (0,)), ((), ()))

    def proj_items(src_ref, src_rows, rows):
        def norm():
            h_scr[rows, :] = _rms(src_ref[0, src_rows, :], gpre_ref[...]).astype(BF16)

        def cols(lo):
            return _dot(h_scr[rows, :], win_ref[lo // MXU_COLS])

        def q_item(pair):
            def f():
                res = cols(pair * MXU_COLS) * (HEAD_K ** -0.5)
                q_scr[2 * pair, rows, :] = res[:, :HEAD_K]
                q_scr[2 * pair + 1, rows, :] = res[:, HEAD_K:]
            return f

        def k_item(pair):
            def f():
                res = cols(DK_TOT + pair * MXU_COLS)
                k_scr[2 * pair, rows, :] = res[:, :HEAD_K]
                k_scr[2 * pair + 1, rows, :] = res[:, HEAD_K:]
            return f

        def v_item(n):
            def f():
                v_scr[rows, n * MXU_COLS:(n + 1) * MXU_COLS] = cols(
                    2 * DK_TOT + n * MXU_COLS).astype(BF16)
            return f

        def r_item(n):
            def f():
                r_scr[rows, n * MXU_COLS:(n + 1) * MXU_COLS] = _silu(cols(
                    2 * DK_TOT + DV_TOT + n * MXU_COLS)).astype(BF16)
            return f

        def gate():
            hf = _dot(h_scr[rows, :], wf1_ref[...]).astype(BF16)
            xg = _dot(hf, wf2_ref[...]) + bf_ref[...]
            g_scr[rows, :] = ((jnp.minimum(xg, 0.0) - jnp.log1p(jnp.exp(-jnp.abs(xg))))
                              * (LOG2_E / GATE_NORM))

        items = [norm, gate]
        items += [q_item(p) for p in range(DK_TOT // MXU_COLS)]
        items += [k_item(p) for p in range(DK_TOT // MXU_COLS)]
        for n in range(DV_TOT // MXU_COLS):
            items += [v_item(n), r_item(n)]
        return items

    def out_items(rows):
        def o_item(n):
            def f():
                y_scr[rows, n * MXU_COLS:(n + 1) * MXU_COLS] = _dot(og_scr[rows, :], wout_ref[n])
            return f

        def post():
            y_ref[0, rows, :] = x_ref[0, rows, :] + _rms(y_scr[rows, :], gpost_ref[...])

        return [o_item(n) for n in range(D_MODEL // MXU_COLS)] + [post]

    def run_chunk(c, queue):
        if isinstance(c, int):
            base = c * chunk
            rows = slice(base, base + chunk)
        else:
            base = pl.multiple_of(c * chunk, chunk)
            rows = pl.ds(base, chunk)
        g = g_scr[rows, :]
        g_hi = g.astype(BF16)
        g_lo = (g - g_hi.astype(F32)).astype(BF16)
        bcum = _dot(tri_ref[...], jnp.concatenate([g_hi, g_lo], axis=0))
        level = level_ref[...]
        queue.tick()
        n_blocks = chunk // SUBLANES
        for hd in range(N_HEADS):
            ks = slice(hd * HEAD_K, (hd + 1) * HEAD_K)
            vs = slice(hd * HEAD_V, (hd + 1) * HEAD_V)
            q = q_scr[hd, rows, :]
            k = k_scr[hd, rows, :]
            b = bcum[:, ks]
            bc_scr[hd] = b
            b_blk = [b[r0:r0 + SUBLANES] for r0 in range(0, chunk, SUBLANES)]
            b_last = _row_bcast(bc_scr, hd, chunk - 1)
            q_in = (q * jnp.exp2(b)).astype(BF16)
            k_st = (k * jnp.exp2(jnp.concatenate([b_last - bb for bb in b_blk], axis=0))).astype(BF16)
            d_last = jnp.exp2(b_last[0:1])

            att = jnp.zeros((chunk, chunk), F32)
            for li in range(nlev):
                half = chunk >> (li + 1)
                src, expo = [], []
                for bi in range(n_blocks):
                    r0 = bi * SUBLANES
                    start = r0 - r0 % (2 * half)
                    b_mid = _row_bcast(bc_scr, hd, start + half - 1)
                    if r0 - start >= half:
                        src.append(q[r0:r0 + SUBLANES])
                        expo.append(b_blk[bi] - b_mid)
                    else:
                        src.append(k[r0:r0 + SUBLANES])
                        expo.append(b_mid - b_blk[bi])
                w = (jnp.concatenate(src, axis=0)
                     * jnp.exp2(jnp.concatenate(expo, axis=0))).astype(BF16)
                att = jnp.where(level == li,
                                lax.dot_general(w, w, nt, preferred_element_type=F32), att)
            queue.tick()

            blocks = []
            for bi in range(n_blocks):
                r0 = bi * SUBLANES
                qb = q[r0:r0 + SUBLANES]
                bb = b_blk[bi]
                blk = jnp.zeros((SUBLANES, LANES), F32)
                for j in range(SUBLANES):
                    k_row = _row_bcast(k_scr, hd, base + r0 + j)
                    b_row = _row_bcast(bc_scr, hd, r0 + j)
                    p = qb * (k_row * jnp.exp2(bb - b_row))
                    blk = jnp.where(lane == j, jnp.sum(p, axis=1, keepdims=True), blk)
                blocks.append(blk)
                if bi % DIAG_BLOCKS_PER_TICK == DIAG_BLOCKS_PER_TICK - 1:
                    queue.tick()
            a_diag = _dot(jnp.concatenate(blocks, axis=0).astype(BF16), spread_ref[...])
            att = jnp.where(level == nlev, a_diag, att)
            queue.tick()

            v = v_scr[rows, vs]
            s = s_scr[hd]
            if chunk % LANES == 0:
                o = _dot(jnp.concatenate([q_in, att.astype(BF16)], axis=1),
                         jnp.concatenate([s.astype(BF16), v], axis=0))
            else:
                o = _dot(q_in, s.astype(BF16)) + _dot(att.astype(BF16), v)
            on = _rms(o, ghead_ref[:, vs])
            og_scr[rows, vs] = (on * r_scr[rows, vs].astype(F32)).astype(BF16)
            d_col = jnp.sum(jnp.where(eye, d_last, 0.0), axis=1, keepdims=True)
            kv = lax.dot_general(k_st, v, tn, preferred_element_type=F32)
            s_scr[hd] = d_col * s + kv

    def run_phase(chunks, items):
        ticks_per_head = 2 + (chunk // SUBLANES) // DIAG_BLOCKS_PER_TICK
        queue = _WorkQueue(items, len(chunks) * (1 + N_HEADS * ticks_per_head))
        for c in chunks:
            run_chunk(c, queue)
        queue.flush()

    def blocks(lo, hi, step):
        return [slice(r, r + step) for r in range(lo, hi, step)]

    if pipelined:
        half = tile // 2
        first, second = blocks(0, half, ROW_BLOCK), blocks(half, tile, ROW_BLOCK)

        @pl.when(i == 0)
        def _():
            s_scr[...] = s0_ref[0]
            for rows in first:
                for item in proj_items(x_ref, rows, rows):
                    item()

        run_phase(range(0, n_chunks // 2),
                  [it for rows in second for it in proj_items(x_ref, rows, rows)])
        run_phase(range(n_chunks // 2, n_chunks),
                  [it for rows in first for it in proj_items(xn_ref, rows, rows)]
                  + [it for rows in first for it in out_items(rows)])
        for rows in second:
            for item in out_items(rows):
                item()
        @pl.when(i == pl.num_programs(1) - 1)
        def _():
            sout_ref[0] = s_scr[...]
    else:
        whole = slice(0, tile)

        @pl.when(i == 0)
        def _():
            for item in proj_items(x_ref, whole, whole):
                item()

        s_scr[...] = s0_ref[0]
        run_phase([i], [])
        sout_ref[0] = s_scr[...]

        @pl.when(i == pl.num_programs(1) - 1)
        def _():
            for item in out_items(whole):
                item()


def _conv_kernel(x_ref, hist_ref, gpre_ref, gpost_ref, win_ref, cw_ref, wout_ref,
                 y_ref, hout_ref,
                 xc_scr,
                 *, tile, seg):
    i = pl.program_id(1)

    if seg is None:
        @pl.when(i == 0)
        def _():
            xc_scr[0:HIST_ROW, :] = hist_ref[0]
    else:
        xc_scr[0:HIST_ROW, :] = jnp.zeros((HIST_ROW, D_INNER), F32)
        pos = lax.broadcasted_iota(jnp.int32, (tile, CONV_COLS), 0) & (seg - 1)

    x = x_ref[0]
    h = _rms(x, gpre_ref[...]).astype(BF16)
    n_out = D_MODEL // MXU_COLS
    acc = [jnp.zeros((tile, MXU_COLS), F32) for _ in range(n_out)]
    n_cb = D_INNER // CONV_COLS
    for cb in range(n_cb):
        cols = slice(cb * CONV_COLS, (cb + 1) * CONV_COLS)

        def proj(part):
            n_w = CONV_COLS // CONV_PIECE
            first = (part * n_cb + cb) * n_w
            return jnp.concatenate([_dot(h, win_ref[first + p]) for p in range(n_w)], axis=1)

        xc = proj(1) * proj(2)
        xc_scr[HIST_ROW:HIST_ROW + tile, cols] = xc
        x2 = xc_scr[HIST_ROW - 2:HIST_ROW - 2 + tile, cols]
        x1 = xc_scr[HIST_ROW - 1:HIST_ROW - 1 + tile, cols]
        if seg is not None:
            x1 = jnp.where(pos < 1, hist_ref[0, 0, :, cols], x1)
            x2 = jnp.where(pos < 2, hist_ref[0, 1, :, cols], x2)
            hout_ref[0, :, cols] = xc
        conv = x2 * cw_ref[0:1, cols] + x1 * cw_ref[1:2, cols] + xc * cw_ref[2:3, cols]
        yb = (proj(0) * conv * _silu(proj(3))).astype(BF16)
        for n in range(n_out):
            acc[n] = acc[n] + _dot(yb, wout_ref[n, cols, :])
        if seg is None:
            xc_scr[0:HIST_ROW, cols] = xc[tile - HIST_ROW:tile]
    y_ref[0] = x + _rms(jnp.concatenate(acc, axis=1), gpost_ref[...])

    if seg is None:
        @pl.when(i == pl.num_programs(1) - 1)
        def _():
            hout_ref[0] = xc_scr[0:HIST_ROW, :]


def _tile_sizes(t_len):
    tile = min(t_len, MAX_TILE)
    chunk = min(tile, MAX_CHUNK)
    assert t_len % tile == 0 and tile % chunk == 0 and chunk % SUBLANES == 0
    return tile, chunk


def _const_spec(shape):
    return pl.BlockSpec(shape, lambda b, i: (0,) * len(shape), pipeline_mode=pl.Buffered(1))


def _layer_spec(stacked, li):
    rest = stacked.shape[1:]
    return pl.BlockSpec((None,) + rest, lambda b, i: (li,) + (0,) * len(rest),
                        pipeline_mode=pl.Buffered(1))


def _cast_kernel(w_ref, o_ref, *, width, group):
    for p in range(group):
        o_ref[0, p] = w_ref[0, :, p * width:(p + 1) * width].astype(BF16)


def _col_pieces(w, width):
    n_layers, k_dim, n_dim = w.shape
    n_pieces = n_dim // width
    group = max(g for g in range(1, n_pieces + 1)
                if n_pieces % g == 0 and g * k_dim * width * 4 <= CAST_BLOCK_BYTES)
    return pl.pallas_call(
        functools.partial(_cast_kernel, width=width, group=group),
        grid=(n_layers, n_pieces // group),
        in_specs=[pl.BlockSpec((1, k_dim, width * group), lambda l, n: (l, 0, n))],
        out_specs=pl.BlockSpec((1, group, k_dim, width), lambda l, n: (l, n, 0, 0)),
        out_shape=jax.ShapeDtypeStruct((n_layers, n_pieces, k_dim, width), BF16),
        compiler_params=pltpu.CompilerParams(dimension_semantics=("arbitrary", "arbitrary")),
        name="weight_pieces",
    )(w)


def _gla_layer(x, s0, s0_li, li, gpre, gpost, win, wf1, wf2, bf, ghead, wout):
    bsz, t_len, _ = x.shape
    batched = t_len <= MAX_CHUNK and bsz * t_len <= MAX_TILE
    if batched:
        tile, chunk = bsz * t_len, t_len
        x_in = x.reshape(1, tile, D_MODEL)
        grid = (1, bsz)
        tok_spec = pl.BlockSpec((1, tile, D_MODEL), lambda b, i: (0, 0, 0))
        st_spec = pl.BlockSpec((1, N_HEADS, HEAD_K, HEAD_V), lambda b, i: (i, 0, 0, 0))
        st_in_spec = pl.BlockSpec((None, 1, N_HEADS, HEAD_K, HEAD_V),
                                  lambda b, i: (s0_li, i, 0, 0, 0))
        n_tiles = 1
    else:
        tile, chunk = _tile_sizes(t_len)
        n_tiles = t_len // tile
        assert (tile // chunk) % 2 == 0 and (tile // 2) % ROW_BLOCK == 0
        x_in = x
        grid = (bsz, n_tiles)
        tok_spec = pl.BlockSpec((1, tile, D_MODEL), lambda b, i: (b, i, 0))
        st_spec = pl.BlockSpec((1, N_HEADS, HEAD_K, HEAD_V), lambda b, i: (b, 0, 0, 0))
        st_in_spec = pl.BlockSpec((None, 1, N_HEADS, HEAD_K, HEAD_V),
                                  lambda b, i: (s0_li, b, 0, 0, 0))
    nlev, tri, level, spread = _chunk_constants(chunk)
    half = tile // 2
    kern = functools.partial(_gla_kernel, tile=tile, chunk=chunk, nlev=nlev,
                             pipelined=not batched)
    next_spec = pl.BlockSpec(
        (1, half, D_MODEL),
        lambda b, i: (0 if batched else b, jnp.minimum(2 * i + 2, 2 * n_tiles - 1), 0))
    y, s_out = pl.pallas_call(
        kern,
        grid=grid,
        in_specs=[
            tok_spec, next_spec, st_in_spec,
            _const_spec((1, D_MODEL)), _const_spec((1, D_MODEL)),
            _layer_spec(win, li),
            _const_spec((D_MODEL, LANES)), _const_spec((LANES, DK_TOT)),
            _const_spec((1, DK_TOT)), _const_spec((1, DV_TOT)),
            _layer_spec(wout, li),
            _const_spec(tri.shape), _const_spec((chunk, chunk)),
            _const_spec((LANES, chunk)),
        ],
        out_specs=[tok_spec, st_spec],
        out_shape=[jax.ShapeDtypeStruct(x_in.shape, F32),
                   jax.ShapeDtypeStruct(s0.shape[1:], F32)],
        scratch_shapes=[
            pltpu.VMEM((N_HEADS, HEAD_K, HEAD_V), F32),
            pltpu.VMEM((tile, D_MODEL), BF16),
            pltpu.VMEM((N_HEADS, tile, HEAD_K), F32),
            pltpu.VMEM((N_HEADS, tile, HEAD_K), F32),
            pltpu.VMEM((tile, DK_TOT), F32),
            pltpu.VMEM((tile, DV_TOT), BF16),
            pltpu.VMEM((tile, DV_TOT), BF16),
            pltpu.VMEM((tile, DV_TOT), BF16),
            pltpu.VMEM((N_HEADS, chunk, HEAD_K), F32),
            pltpu.VMEM((tile, D_MODEL), F32),
        ],
        compiler_params=pltpu.CompilerParams(
            dimension_semantics=("arbitrary", "arbitrary"),
            vmem_limit_bytes=VMEM_LIMIT_BYTES),
        name="gla_layer",
    )(x_in, x_in, s0, gpre, gpost, win, wf1, wf2, bf, ghead, wout, tri, level, spread)
    return y.reshape(x.shape), s_out


def _conv_layer(x, hist, li, gpre, gpost, win, cw, wout):
    bsz, t_len, _ = x.shape
    n_hist = CONV_W - 1
    segmented = bsz * t_len <= MAX_TILE and t_len & (t_len - 1) == 0 and t_len >= n_hist
    if segmented:
        tile = bsz * t_len
        taps = [jnp.pad(hist[:, n_hist - d:], ((0, 0), (0, t_len - d), (0, 0))).reshape(tile, D_INNER)
                for d in range(1, CONV_W)]
        hist_in = jnp.stack(taps)[None]
        x_in = x.reshape(1, tile, D_MODEL)
        grid = (1, 1)
        hist_spec = pl.BlockSpec((1, n_hist, tile, D_INNER), lambda b, i: (0, 0, 0, 0))
        hout_spec = pl.BlockSpec((1, tile, D_INNER), lambda b, i: (0, 0, 0))
        hout_shape = (1, tile, D_INNER)
        seg = t_len
    else:
        tile, _ = _tile_sizes(t_len)
        assert tile >= HIST_ROW
        hist_in = jnp.pad(hist, ((0, 0), (HIST_ROW - n_hist, 0), (0, 0)))
        x_in = x
        grid = (bsz, t_len // tile)
        hist_spec = hout_spec = pl.BlockSpec((1, HIST_ROW, D_INNER), lambda b, i: (b, 0, 0))
        hout_shape = (bsz, HIST_ROW, D_INNER)
        seg = None
    kern = functools.partial(_conv_kernel, tile=tile, seg=seg)
    tok_spec = pl.BlockSpec((1, tile, D_MODEL), lambda b, i: (b, i, 0))
    y, hout = pl.pallas_call(
        kern,
        grid=grid,
        in_specs=[
            tok_spec, hist_spec,
            _const_spec((1, D_MODEL)), _const_spec((1, D_MODEL)),
            _layer_spec(win, li),
            _const_spec((CONV_W, D_INNER)),
            _layer_spec(wout, li),
        ],
        out_specs=[tok_spec, hout_spec],
        out_shape=[jax.ShapeDtypeStruct(x_in.shape, F32),
                   jax.ShapeDtypeStruct(hout_shape, F32)],
        scratch_shapes=[pltpu.VMEM((HIST_ROW + tile, D_INNER), F32)],
        compiler_params=pltpu.CompilerParams(
            dimension_semantics=("arbitrary", "arbitrary"),
            vmem_limit_bytes=VMEM_LIMIT_BYTES),
        name="conv_layer",
    )(x_in, hist_in, gpre, gpost, win, cw, wout)
    if segmented:
        return (y.reshape(x.shape),
                hout.reshape(bsz, t_len, D_INNER)[:, t_len - n_hist:])
    return y, hout[:, HIST_ROW - n_hist:]


def kernel(x_prompt, x_sample, state_gla, state_conv, norm_pre, norm_post, gla_w_in, gla_w_f1, gla_w_f2, gla_b_f, gla_g_head, gla_w_out, conv_w_in, conv_w, conv_w_out):
    depth = norm_pre.shape[0]
    xp, xs = x_prompt, x_sample
    s0_prompt = jnp.zeros((1, xp.shape[0], N_HEADS, HEAD_K, HEAD_V), F32)
    hist_prompt = jnp.zeros((xp.shape[0], CONV_W - 1, D_INNER), F32)
    gla_p, gla_s, conv_p, conv_s = [], [], [], []
    gla_win = _col_pieces(gla_w_in, MXU_COLS)
    gla_wout = _col_pieces(gla_w_out, MXU_COLS)
    conv_win = _col_pieces(conv_w_in, CONV_PIECE)
    conv_wout = _col_pieces(conv_w_out, MXU_COLS)
    for layer in range(depth):
        li = layer // 2
        gpre = norm_pre[layer][None]
        gpost = norm_post[layer][None]
        if layer % 2 == 0:
            rank = gla_w_f1.shape[-1]
            prm = (gpre, gpost,
                   gla_win,
                   jnp.pad(gla_w_f1[li], ((0, 0), (0, LANES - rank))).astype(BF16),
                   jnp.pad(gla_w_f2[li], ((0, LANES - rank), (0, 0))).astype(BF16),
                   gla_b_f[li][None], gla_g_head[li][None],
                   gla_wout)
            xp, sp = _gla_layer(xp, s0_prompt, 0, li, *prm)
            xs, ss = _gla_layer(xs, state_gla, li, li, *prm)
            gla_p.append(sp)
            gla_s.append(ss)
        else:
            prm = (gpre, gpost, conv_win, conv_w[li], conv_wout)
            xp, sp = _conv_layer(xp, hist_prompt, li, *prm)
            xs, ss = _conv_layer(xs, state_conv[li], li, *prm)
            conv_p.append(sp)
            conv_s.append(ss)
    return (xp, xs, jnp.stack(gla_p), jnp.stack(conv_p), jnp.stack(gla_s), jnp.stack(conv_s))
```
